```python
import jax
import jax.numpy as jnp
from jax import lax
import numpy as np

D_MODEL = 4096
BATCH = 32
SEQ = 256
DEPTH = 1
DEC_BATCH = 4
DEC_SEQ = 2048
PAST_LEN = 512

GRID_W = 64
N_HEADS = 16
N_KV_HEADS = 4
GQA_GROUP = N_HEADS // N_KV_HEADS
HEAD_DIM = 128
WINDOW = 128
ATT_BLOCK = 128
ROPE_BASE = 10000.0
RET_HEADS = 16
RET_DK = 128
RET_DV = 128
RET_CHUNK = 128
PEER_HEADS = 8
PEER_DKEY = 256
N_KEYS = 128
N_EXPERTS = N_KEYS * N_KEYS
PEER_TOPK = 16
PEER_BLOCK = 128
NORM_EPS = 1e-6

Q_W = N_HEADS * HEAD_DIM
KV_W = N_KV_HEADS * HEAD_DIM
RQK_W = RET_HEADS * RET_DK
RV_W = RET_HEADS * RET_DV
IN_SPLITS = (Q_W, KV_W, KV_W, RQK_W, RQK_W, RV_W, RV_W, RV_W, D_MODEL, D_MODEL)
IN_WIDTH = Q_W + 2 * KV_W + 2 * RQK_W + 3 * RV_W + 2 * D_MODEL

kernel_name = 'hybrid_diffusion_swa_retention_peer_step'


def rmsnorm(x, g):
    xf = x.astype(jnp.float32)
    y = xf * lax.rsqrt(jnp.mean(xf * xf, axis=-1, keepdims=True) + NORM_EPS)
    return (y * g.astype(jnp.float32)).astype(x.dtype)


def ada_modulation(cond, w_ada, b_ada):
    m = jax.nn.silu(cond) @ w_ada + b_ada
    return jnp.split(m, 6, axis=-1)


def modulate(xn, shift, scale):
    return xn * (1.0 + scale[:, None, :]) + shift[:, None, :]


def combined_projection(h, w_in):
    pts, acc = [], 0
    for w in IN_SPLITS[:-1]:
        acc += w
        pts.append(acc)
    return jnp.split(h @ w_in, pts, axis=-1)


def axial_rope_tables(n_tokens):
    rows = n_tokens // GRID_W
    row = jnp.repeat(jnp.arange(rows), GRID_W).astype(jnp.float32)
    col = jnp.tile(jnp.arange(GRID_W), rows).astype(jnp.float32)
    half = HEAD_DIM // 2
    inv = 1.0 / (ROPE_BASE ** (jnp.arange(0, half, 2, dtype=jnp.float32) / half))
    ar = row[:, None] * inv[None, :]
    ac = col[:, None] * inv[None, :]
    ang = jnp.concatenate([ar, ar, ac, ac], axis=-1)
    return jnp.cos(ang), jnp.sin(ang)


def apply_axial_rope(x, cos, sin):
    xf = x.astype(jnp.float32)
    x1, x2, x3, x4 = jnp.split(xf, 4, axis=-1)
    rot = jnp.concatenate([-x2, x1, -x4, x3], axis=-1)
    return (xf * cos[None, :, None, :] + rot * sin[None, :, None, :]).astype(x.dtype)


def softmax_with_sink(logits, sink):
    s = sink[None, :, :, None, None]
    m = jnp.maximum(jnp.max(logits, axis=-1, keepdims=True), s)
    p = jnp.exp(logits - m)
    return p / (jnp.sum(p, axis=-1, keepdims=True) + jnp.exp(s - m))


def to_query_blocks(q):
    b, t = q.shape[:2]
    qb = q.reshape(b, t // ATT_BLOCK, ATT_BLOCK, N_KV_HEADS, GQA_GROUP, HEAD_DIM)
    return jnp.moveaxis(qb, 1, 0)


def from_query_blocks(ob):
    nb, b = ob.shape[:2]
    return jnp.moveaxis(ob, 0, 1).reshape(b, nb * ATT_BLOCK, Q_W)


def context_attention(q, k, v, sink):
    scale = HEAD_DIM ** -0.5
    vf = v.astype(jnp.float32)

    def block(qn):
        lg = jnp.einsum('bqkgd,bskd->bkgqs', qn, k).astype(jnp.float32) * scale
        p = softmax_with_sink(lg, sink)
        return jnp.einsum('bkgqs,bskd->bqkgd', p, vf).astype(q.dtype)

    return from_query_blocks(lax.map(block, to_query_blocks(q)))


def latent_attention(q, k, v, k_ctx, v_ctx, sink):
    t = k.shape[1]
    span = ATT_BLOCK + 2 * WINDOW
    scale = HEAD_DIM ** -0.5
    pad = ((0, 0), (WINDOW, WINDOW), (0, 0), (0, 0))
    k_pad = jnp.pad(k, pad)
    v_pad = jnp.pad(v, pad)
    offs_q = jnp.arange(ATT_BLOCK)
    offs_k = jnp.arange(span)
    v_ctx_f = v_ctx.astype(jnp.float32)

    def block(args):
        n, qn = args
        start = n * ATT_BLOCK
        kw = lax.dynamic_slice_in_dim(k_pad, start, span, axis=1)
        vw = lax.dynamic_slice_in_dim(v_pad, start, span, axis=1)
        qpos = start + offs_q
        kpos = start - WINDOW + offs_k
        valid = (jnp.abs(qpos[:, None] - kpos[None, :]) <= WINDOW) & (kpos[None, :] >= 0) & (kpos[None, :] < t)
        lw = jnp.einsum('bqkgd,bskd->bkgqs', qn, kw).astype(jnp.float32) * scale
        lw = jnp.where(valid, lw, -jnp.inf)
        lc = jnp.einsum('bqkgd,bskd->bkgqs', qn, k_ctx).astype(jnp.float32) * scale
        p = softmax_with_sink(jnp.concatenate([lw, lc], axis=-1), sink)
        o = (jnp.einsum('bkgqs,bskd->bqkgd', p[..., :span], vw.astype(jnp.float32))
             + jnp.einsum('bkgqs,bskd->bqkgd', p[..., span:], v_ctx_f))
        return o.astype(q.dtype)

    qb = to_query_blocks(q)
    return from_query_blocks(lax.map(block, (jnp.arange(qb.shape[0]), qb)))


def retention_scan(q, k, v, log_gamma, s0):
    b, t, h, dk = q.shape
    dv = v.shape[-1]
    n = t // RET_CHUNK
    qc = q.astype(jnp.float32).reshape(b, n, RET_CHUNK, h, dk)
    kc = k.astype(jnp.float32).reshape(b, n, RET_CHUNK, h, dk)
    vc = v.astype(jnp.float32).reshape(b, n, RET_CHUNK, h, dv)
    idx = jnp.arange(RET_CHUNK, dtype=jnp.float32)
    diff = idx[:, None] - idx[None, :]
    lower = diff >= 0
    decay = jnp.where(lower[None], jnp.exp(jnp.where(lower, diff, 0.0)[None] * log_gamma[:, None, None]), 0.0)
    scores = jnp.einsum('bnihd,bnjhd->bnhij', qc, kc) * decay[None, None]
    o_intra = jnp.einsum('bnhij,bnjhe->bnihe', scores, vc)
    q_decay = jnp.exp((idx[:, None] + 1.0) * log_gamma[None, :])
    k_decay = jnp.exp((RET_CHUNK - 1.0 - idx)[:, None] * log_gamma[None, :])
    kv = jnp.einsum('bnjhd,jh,bnjhe->nbhde', kc, k_decay, vc)
    chunk_decay = jnp.exp(RET_CHUNK * log_gamma)[None, :, None, None]

    def step(state, kv_n):
        return chunk_decay * state + kv_n, state

    s_final, s_starts = lax.scan(step, s0.astype(jnp.float32), kv)
    o_cross = jnp.einsum('bnihd,ih,nbhde->bnihe', qc, q_decay, s_starts)
    return (o_intra + o_cross).reshape(b, t, h, dv), s_final


def head_norm(o):
    b, t = o.shape[:2]
    y = o * lax.rsqrt(jnp.mean(o * o, axis=-1, keepdims=True) + NORM_EPS)
    return y.reshape(b, t, RV_W)


def retention_branch(qr, kr, vr, g_f, g_b, decay_logit, s0_f, s0_b):
    b, t, _ = qr.shape
    q = qr.reshape(b, t, RET_HEADS, RET_DK)
    k = kr.reshape(b, t, RET_HEADS, RET_DK) * (RET_DK ** -0.5)
    v = vr.reshape(b, t, RET_HEADS, RET_DV)
    log_gamma = jax.nn.log_sigmoid(decay_logit.astype(jnp.float32))
    o_f, s_f = retention_scan(q, k, v, log_gamma[0], s0_f)
    o_b, s_b = retention_scan(jnp.flip(q, 1), jnp.flip(k, 1), jnp.flip(v, 1), log_gamma[1], s0_b)
    o_b = jnp.flip(o_b, 1)
    out = (head_norm(o_f) * jax.nn.silu(g_f.astype(jnp.float32))
           + head_norm(o_b) * jax.nn.silu(g_b.astype(jnp.float32)))
    return out.astype(qr.dtype), s_f, s_b


def merge_branches(o_att, o_ret, gate_a, gate_r, w_proj_attn, w_proj_ret, w_out):
    merged = jax.nn.sigmoid(gate_a) * (o_att @ w_proj_attn) + jax.nn.sigmoid(gate_r) * (o_ret @ w_proj_ret)
    return merged @ w_out


def mixer_context(h, w_in, sink, decay_logit, w_proj_attn, w_proj_ret, w_out):
    b, t, _ = h.shape
    q, k, v, qr, kr, vr, g_f, g_b, gate_a, gate_r = combined_projection(h, w_in)
    q = q.reshape(b, t, N_HEADS, HEAD_DIM)
    k = k.reshape(b, t, N_KV_HEADS, HEAD_DIM)
    v = v.reshape(b, t, N_KV_HEADS, HEAD_DIM)
    o_att = context_attention(q, k, v, sink)
    zero_state = jnp.zeros((b, RET_HEADS, RET_DK, RET_DV), jnp.float32)
    o_ret, s_f, s_b = retention_branch(qr, kr, vr, g_f, g_b, decay_logit, zero_state, zero_state)
    out = merge_branches(o_att, o_ret, gate_a, gate_r, w_proj_attn, w_proj_ret, w_out)
    return out, k, v, s_f, s_b


def mixer_latent(h, k_ctx, v_ctx, s_f, s_b, cos, sin, w_in, sink, decay_logit, w_proj_attn, w_proj_ret, w_out):
    b, t, _ = h.shape
    q, k, v, qr, kr, vr, g_f, g_b, gate_a, gate_r = combined_projection(h, w_in)
    q = apply_axial_rope(q.reshape(b, t, N_HEADS, HEAD_DIM), cos, sin)
    k = apply_axial_rope(k.reshape(b, t, N_KV_HEADS, HEAD_DIM), cos, sin)
    v = v.reshape(b, t, N_KV_HEADS, HEAD_DIM)
    o_att = latent_attention(q, k, v, k_ctx, v_ctx, sink)
    o_ret, _, _ = retention_branch(qr, kr, vr, g_f, g_b, decay_logit, s_f, s_b)
    return merge_branches(o_att, o_ret, gate_a, gate_r, w_proj_attn, w_proj_ret, w_out)


def peer(h, w_query, sub_keys, expert_u, expert_v):
    b, t, d = h.shape
    n_tok = b * t
    x = h.reshape(n_tok, d)
    half = PEER_DKEY // 2
    q = jnp.einsum('nd,dhk->nhk', x, w_query).astype(jnp.float32)
    s1 = jnp.einsum('nhk,hek->nhe', q[..., :half], sub_keys[:, 0].astype(jnp.float32))
    s2 = jnp.einsum('nhk,hek->nhe', q[..., half:], sub_keys[:, 1].astype(jnp.float32))
    v1, i1 = lax.top_k(s1, PEER_TOPK)
    v2, i2 = lax.top_k(s2, PEER_TOPK)
    cand = (v1[..., :, None] + v2[..., None, :]).reshape(n_tok, PEER_HEADS, PEER_TOPK * PEER_TOPK)
    cidx = (i1[..., :, None] * N_KEYS + i2[..., None, :]).reshape(n_tok, PEER_HEADS, PEER_TOPK * PEER_TOPK)
    best, pos = lax.top_k(cand, PEER_TOPK)
    eidx = jnp.take_along_axis(cidx, pos, axis=-1)
    g = jax.nn.softmax(best, axis=-1)
    nb = n_tok // PEER_BLOCK

    def block(args):
        xb, eb, gb = args
        u = expert_u[eb]
        a = jax.nn.gelu(jnp.einsum('td,thkd->thk', xb, u).astype(jnp.float32), approximate=False)
        coef = (gb * a).astype(xb.dtype)
        return jnp.einsum('thk,thkd->td', coef, expert_v[eb])

    out = lax.map(block, (x.reshape(nb, PEER_BLOCK, d),
                          eidx.reshape(nb, PEER_BLOCK, PEER_HEADS, PEER_TOPK),
                          g.reshape(nb, PEER_BLOCK, PEER_HEADS, PEER_TOPK)))
    return out.reshape(b, t, d).astype(h.dtype)


def setup_inputs(seed: int = 0) -> dict:
    key = jax.random.key(seed)
    ks = jax.random.split(key, 24)
    f32 = jnp.float32

    def nrm(k, shape, scale):
        return jax.random.normal(k, shape, f32) * scale

    a = 5.0 + jnp.arange(RET_HEADS, dtype=f32)
    gamma0 = 1.0 - 2.0 ** (-a)
    base_logit = jnp.log(gamma0) + a * jnp.log(2.0)
    state_scale = (PAST_LEN ** 0.5) * (RET_DK ** -0.5)
    return {
        'x_prompt': nrm(ks[0], (BATCH, SEQ, D_MODEL), 1.0),
        'x_sample': nrm(ks[1], (DEC_BATCH, DEC_SEQ, D_MODEL), 1.0),
        'cache_attn_k': nrm(ks[2], (DEC_BATCH, DEPTH, PAST_LEN, N_KV_HEADS, HEAD_DIM), 1.0),
        'cache_attn_v': nrm(ks[3], (DEC_BATCH, DEPTH, PAST_LEN, N_KV_HEADS, HEAD_DIM), 1.0),
        'state_ret_fwd': nrm(ks[4], (DEC_BATCH, DEPTH, RET_HEADS, RET_DK, RET_DV), state_scale),
        'state_ret_bwd': nrm(ks[5], (DEC_BATCH, DEPTH, RET_HEADS, RET_DK, RET_DV), state_scale),
        'c': nrm(ks[6], (DEC_BATCH, D_MODEL), 1.0),
        'c_ctx': nrm(ks[7], (D_MODEL,), 1.0),
        'w_ada': nrm(ks[8], (DEPTH, D_MODEL, 6 * D_MODEL), 0.5 * D_MODEL ** -0.5),
        'b_ada': nrm(ks[9], (DEPTH, 6 * D_MODEL), 0.02),
        'norm_mix': 1.0 + nrm(ks[10], (DEPTH, D_MODEL), 0.02),
        'norm_ffn': 1.0 + nrm(ks[11], (DEPTH, D_MODEL), 0.02),
        'w_in': nrm(ks[12], (DEPTH, D_MODEL, IN_WIDTH), D_MODEL ** -0.5),
        'attn_sink': nrm(ks[13], (DEPTH, N_HEADS), 0.5),
        'ret_decay_logit': base_logit[None, None, :] + nrm(ks[14], (DEPTH, 2, RET_HEADS), 0.1),
        'w_proj_attn': nrm(ks[15], (DEPTH, Q_W, D_MODEL), Q_W ** -0.5),
        'w_proj_ret': nrm(ks[16], (DEPTH, RV_W, D_MODEL), RV_W ** -0.5),
        'w_out': nrm(ks[17], (DEPTH, D_MODEL, D_MODEL), D_MODEL ** -0.5),
        'peer_w_query': nrm(ks[18], (DEPTH, D_MODEL, PEER_HEADS, PEER_DKEY), D_MODEL ** -0.5),
        'peer_sub_keys': nrm(ks[19], (DEPTH, PEER_HEADS, 2, N_KEYS, PEER_DKEY // 2), (PEER_DKEY // 2) ** -0.5),
        'peer_u': nrm(ks[20], (DEPTH, N_EXPERTS, D_MODEL), D_MODEL ** -0.5),
        'peer_v': nrm(ks[21], (DEPTH, N_EXPERTS, D_MODEL), PEER_HEADS ** -0.5),
        'norm_final': 1.0 + nrm(ks[22], (D_MODEL,), 0.02),
    }


def reference(x_prompt, x_sample, cache_attn_k, cache_attn_v, state_ret_fwd, state_ret_bwd, c, c_ctx,
              w_ada, b_ada, norm_mix, norm_ffn, w_in, attn_sink, ret_decay_logit, w_proj_attn, w_proj_ret,
              w_out, peer_w_query, peer_sub_keys, peer_u, peer_v, norm_final):
    xp = x_prompt
    xs = x_sample
    cos, sin = axial_rope_tables(xs.shape[1])
    new_k, new_v, new_sf, new_sb = [], [], [], []
    for l in range(DEPTH):
        sink = attn_sink[l].astype(jnp.float32).reshape(N_KV_HEADS, GQA_GROUP)
        sh1, sc1, g1, sh2, sc2, g2 = ada_modulation(c_ctx[None, :], w_ada[l], b_ada[l])
        h = modulate(rmsnorm(xp, norm_mix[l]), sh1, sc1)
        mo, k_c, v_c, s_f, s_b = mixer_context(h, w_in[l], sink, ret_decay_logit[l],
                                               w_proj_attn[l], w_proj_ret[l], w_out[l])
        xp = xp + g1[:, None, :] * mo
        h = modulate(rmsnorm(xp, norm_ffn[l]), sh2, sc2)
        xp = xp + g2[:, None, :] * peer(h, peer_w_query[l], peer_sub_keys[l], peer_u[l], peer_v[l])
        new_k.append(k_c)
        new_v.append(v_c)
        new_sf.append(s_f.astype(xp.dtype))
        new_sb.append(s_b.astype(xp.dtype))
        sh1, sc1, g1, sh2, sc2, g2 = ada_modulation(c, w_ada[l], b_ada[l])
        h = modulate(rmsnorm(xs, norm_mix[l]), sh1, sc1)
        mo = mixer_latent(h, cache_attn_k[:, l], cache_attn_v[:, l], state_ret_fwd[:, l], state_ret_bwd[:, l],
                          cos, sin, w_in[l], sink, ret_decay_logit[l], w_proj_attn[l], w_proj_ret[l], w_out[l])
        xs = xs + g1[:, None, :] * mo
        h = modulate(rmsnorm(xs, norm_ffn[l]), sh2, sc2)
        xs = xs + g2[:, None, :] * peer(h, peer_w_query[l], peer_sub_keys[l], peer_u[l], peer_v[l])
    y_prompt = rmsnorm(xp, norm_final)
    y_sample = rmsnorm(xs, norm_final)
    new_attn_k = jnp.stack(new_k, axis=1)
    new_attn_v = jnp.stack(new_v, axis=1)
    new_ret_fwd = jnp.stack(new_sf, axis=1)
    new_ret_bwd = jnp.stack(new_sb, axis=1)
    return (y_prompt, y_sample, new_attn_k, new_attn_v, new_ret_fwd, new_ret_bwd)
```

```python
import functools

import jax
import jax.numpy as jnp
from jax import lax
from jax.experimental import pallas as pl
from jax.experimental.pallas import tpu as pltpu

GRID_W = 64
WINDOW = 128
ATT_BLOCK = 128
ROPE_BASE = 10000.0
RET_CHUNK = 128
PEER_TOPK = 16
NORM_EPS = 1e-6

LANES = 128
VMEM_LIMIT_BYTES = 56 * 1024 * 1024
NEG_BIG = -1e30

f32 = jnp.float32
bf16 = jnp.bfloat16


def _params(*sem):
    return pltpu.CompilerParams(dimension_semantics=sem, vmem_limit_bytes=VMEM_LIMIT_BYTES)


def _tile(pref, *dims):
    t = pref
    while any(dim % t for dim in dims):
        t //= 2
    assert t >= LANES
    return t


def _dot(a, b):
    return jnp.dot(a, b, preferred_element_type=f32)


def _dot_nt(a, b):
    return lax.dot_general(a, b, (((1,), (1,)), ((), ())), preferred_element_type=f32)


def _ada_kernel(cond_ref, w_ref, b_ref, o_ref):
    c = cond_ref[...]
    s = (c * jax.nn.sigmoid(c)).astype(bf16)
    o_ref[...] = _dot(s, w_ref[...].astype(bf16)) + b_ref[...]


def ada_modulation(cond, w_ada, b_ada):
    r, d = cond.shape
    n = w_ada.shape[1]
    tn = _tile(512, n)
    return pl.pallas_call(
        _ada_kernel,
        grid=(n // tn,),
        in_specs=[pl.BlockSpec((r, d), lambda j: (0, 0)),
                  pl.BlockSpec((d, tn), lambda j: (0, j)),
                  pl.BlockSpec((1, tn), lambda j: (0, j))],
        out_specs=pl.BlockSpec((r, tn), lambda j: (0, j)),
        out_shape=jax.ShapeDtypeStruct((r, n), f32),
        compiler_params=_params("parallel"),
        name="ada_modulation",
    )(cond, w_ada, b_ada.reshape(1, n))


def _norm_mod(x, g, sh, sc):
    y = x * lax.rsqrt(jnp.mean(x * x, axis=-1, keepdims=True) + NORM_EPS)
    y = y * g
    return y * (1.0 + sc) + sh


def _norm_mod_matmul_kernel(x_ref, g_ref, sh_ref, sc_ref, w_ref, o_ref, h_scr):
    @pl.when(pl.program_id(1) == 0)
    def _():
        h_scr[...] = _norm_mod(x_ref[...], g_ref[...], sh_ref[0], sc_ref[0]).astype(bf16)

    o_ref[...] = _dot(h_scr[...], w_ref[...])


def _norm_mod_matmul_h_kernel(x_ref, g_ref, sh_ref, sc_ref, w_ref, o_ref, h_ref):
    @pl.when(pl.program_id(1) == 0)
    def _():
        h_ref[...] = _norm_mod(x_ref[...], g_ref[...], sh_ref[0], sc_ref[0]).astype(bf16)

    o_ref[...] = _dot(h_ref[...], w_ref[...])


def norm_mod_matmul(x, g, mod3, row_of_tile, k_shift, k_scale, w, tm, tn, emit_h):
    ntok, d = x.shape
    n = w.shape[1]

    def mod_spec(k):
        return pl.BlockSpec((1, 1, d), lambda i, j: (row_of_tile(i, tm) * 6 + k, 0, 0))

    in_specs = [pl.BlockSpec((tm, d), lambda i, j: (i, 0)),
                pl.BlockSpec((1, d), lambda i, j: (0, 0)),
                mod_spec(k_shift), mod_spec(k_scale),
                pl.BlockSpec((d, tn), lambda i, j: (0, j))]
    o_spec = pl.BlockSpec((tm, tn), lambda i, j: (i, j))
    if emit_h:
        return pl.pallas_call(
            _norm_mod_matmul_h_kernel,
            grid=(ntok // tm, n // tn),
            in_specs=in_specs,
            out_specs=[o_spec, pl.BlockSpec((tm, d), lambda i, j: (i, 0))],
            out_shape=[jax.ShapeDtypeStruct((ntok, n), f32), jax.ShapeDtypeStruct((ntok, d), bf16)],
            compiler_params=_params("parallel", "arbitrary"),
            name="norm_mod_matmul_h",
        )(x, g.reshape(1, d), mod3, mod3, w)
    return pl.pallas_call(
        _norm_mod_matmul_kernel,
        grid=(ntok // tm, n // tn),
        in_specs=in_specs,
        out_specs=o_spec,
        out_shape=jax.ShapeDtypeStruct((ntok, n), f32),
        scratch_shapes=[pltpu.VMEM((tm, d), bf16)],
        compiler_params=_params("parallel", "arbitrary"),
        name="norm_mod_matmul",
    )(x, g.reshape(1, d), mod3, mod3, w)


def _softmax_pv(logits, values, sink):
    m = sink
    for lg in logits:
        m = jnp.maximum(m, jnp.max(lg, axis=-1, keepdims=True))
    den = jnp.exp(sink - m)
    acc = None
    for lg, v in zip(logits, values):
        p = jnp.exp(lg - m)
        den = den + jnp.sum(p, axis=-1, keepdims=True)
        pv = _dot(p.astype(bf16), v)
        acc = pv if acc is None else acc + pv
    return acc / den


def _ctx_attn_kernel(sink_ref, q_ref, k_ref, v_ref, _, o_ref, *, group, hd):
    kvh = pl.program_id(1)
    scale = hd ** -0.5
    k = k_ref[...].astype(bf16)
    v = v_ref[...].astype(bf16)
    for g in range(group):
        q = q_ref[:, g * hd:(g + 1) * hd].astype(bf16)
        lg = _dot_nt(q, k) * scale
        o = _softmax_pv([lg], [v], sink_ref[kvh * group + g])
        o_ref[:, g * hd:(g + 1) * hd] = o.astype(o_ref.dtype)


def context_attention(proj, sink, o_att, nb, seq, n_heads, n_kv, hd):
    group = n_heads // n_kv
    q_w = n_heads * hd
    kcol = q_w // hd
    vcol = (q_w + n_kv * hd) // hd
    return pl.pallas_call(
        functools.partial(_ctx_attn_kernel, group=group, hd=hd),
        grid=(nb, n_kv),
        in_specs=[pl.BlockSpec(memory_space=pltpu.SMEM),
                  pl.BlockSpec((seq, group * hd), lambda b, h: (b, h)),
                  pl.BlockSpec((seq, hd), lambda b, h: (b, kcol + h)),
                  pl.BlockSpec((seq, hd), lambda b, h: (b, vcol + h)),
                  pl.BlockSpec(memory_space=pl.ANY)],
        out_specs=pl.BlockSpec((seq, group * hd), lambda b, h: (b, h)),
        out_shape=jax.ShapeDtypeStruct(o_att.shape, o_att.dtype),
        input_output_aliases={4: 0},
        compiler_params=_params("parallel", "parallel"),
        name="context_attention",
    )(sink, proj, proj, proj, o_att)


def _rope(x, cos, sin, first_half):
    rot = jnp.where(first_half, -pltpu.roll(x, 3 * (LANES // 4), 1), pltpu.roll(x, LANES // 4, 1))
    return x * cos + rot * sin


def _lat_attn_kernel(sink_ref, q_ref, kp_ref, kc_ref, kn_ref, vp_ref, vc_ref, vn_ref, kx_ref, vx_ref,
                     cp_ref, cc_ref, cn_ref, sp_ref, sc_ref, sn_ref, _, o_ref, *, group, hd):
    n = pl.program_id(1)
    kvh = pl.program_id(2)
    nblk = pl.num_programs(1)
    scale = hd ** -0.5
    blk = (ATT_BLOCK, hd)
    lane = lax.broadcasted_iota(jnp.int32, blk, 1)
    first_half = (lane % (hd // 2)) < (hd // 4)
    row = lax.broadcasted_iota(jnp.int32, (ATT_BLOCK, ATT_BLOCK), 0)
    col = lax.broadcasted_iota(jnp.int32, (ATT_BLOCK, ATT_BLOCK), 1)
    mask_p = jnp.logical_and(col >= row, n > 0)
    mask_n = jnp.logical_and(col <= row, n < nblk - 1)
    kp = _rope(kp_ref[...], cp_ref[...], sp_ref[...], first_half).astype(bf16)
    kc = _rope(kc_ref[...], cc_ref[...], sc_ref[...], first_half).astype(bf16)
    kn = _rope(kn_ref[...], cn_ref[...], sn_ref[...], first_half).astype(bf16)
    kx = kx_ref[0].astype(bf16)
    values = [vp_ref[...].astype(bf16), vc_ref[...].astype(bf16), vn_ref[...].astype(bf16), vx_ref[0].astype(bf16)]
    for g in range(group):
        q = _rope(q_ref[:, g * hd:(g + 1) * hd], cc_ref[...], sc_ref[...], first_half).astype(bf16)
        lp = jnp.where(mask_p, _dot_nt(q, kp) * scale, NEG_BIG)
        lc = _dot_nt(q, kc) * scale
        ln = jnp.where(mask_n, _dot_nt(q, kn) * scale, NEG_BIG)
        lx = _dot_nt(q, kx) * scale
        o = _softmax_pv([lp, lc, ln, lx], values, sink_ref[kvh * group + g])
        o_ref[:, g * hd:(g + 1) * hd] = o.astype(o_ref.dtype)


def latent_attention(proj, sink, k_ctx, v_ctx, cos, sin, o_att, row0, nb, seq, n_heads, n_kv, hd):
    assert WINDOW == ATT_BLOCK
    group = n_heads // n_kv
    q_w = n_heads * hd
    kcol = q_w // hd
    vcol = (q_w + n_kv * hd) // hd
    nblk = seq // ATT_BLOCK
    past = k_ctx.shape[1]
    r0 = row0 // ATT_BLOCK

    def rb(b, n):
        return r0 + b * nblk + n

    def prev(n):
        return jnp.maximum(n - 1, 0)

    def nxt(n):
        return jnp.minimum(n + 1, nblk - 1)

    def kv_spec(col0, f):
        return pl.BlockSpec((ATT_BLOCK, hd), lambda b, n, h: (rb(b, f(n)), col0 + h))

    def tab_spec(f):
        return pl.BlockSpec((ATT_BLOCK, hd), lambda b, n, h: (f(n), 0))

    ident = lambda n: n
    ctx_spec = pl.BlockSpec((1, past, hd), lambda b, n, h: (b, 0, h))
    return pl.pallas_call(
        functools.partial(_lat_attn_kernel, group=group, hd=hd),
        grid=(nb, nblk, n_kv),
        in_specs=[pl.BlockSpec(memory_space=pltpu.SMEM),
                  pl.BlockSpec((ATT_BLOCK, group * hd), lambda b, n, h: (rb(b, n), h)),
                  kv_spec(kcol, prev), kv_spec(kcol, ident), kv_spec(kcol, nxt),
                  kv_spec(vcol, prev), kv_spec(vcol, ident), kv_spec(vcol, nxt),
                  ctx_spec, ctx_spec,
                  tab_spec(prev), tab_spec(ident), tab_spec(nxt),
                  tab_spec(prev), tab_spec(ident), tab_spec(nxt),
                  pl.BlockSpec(memory_space=pl.ANY)],
        out_specs=pl.BlockSpec((ATT_BLOCK, group * hd), lambda b, n, h: (rb(b, n), h)),
        out_shape=jax.ShapeDtypeStruct(o_att.shape, o_att.dtype),
        input_output_aliases={16: 0},
        compiler_params=_params("parallel", "parallel", "parallel"),
        name="latent_attention",
    )(sink, proj, proj, proj, proj, proj, proj, proj, k_ctx, v_ctx, cos, cos, cos, sin, sin, sin, o_att)


def _ret_kernel(lg_ref, q_ref, k_ref, v_ref, gf_ref, gb_ref, s0f_ref, s0b_ref, _, o_ref, sf_ref, sb_ref,
                of_scr, ob_scr, *, n_chunks, dk):
    h = pl.program_id(1)
    c = RET_CHUNK
    lgf = lg_ref[0, h]
    lgb = lg_ref[1, h]
    ii = lax.broadcasted_iota(jnp.int32, (c, c), 0).astype(f32)
    jj = lax.broadcasted_iota(jnp.int32, (c, c), 1).astype(f32)
    d = ii - jj
    decay_f = jnp.where(d >= 0, jnp.exp(jnp.maximum(d, 0.0) * lgf), 0.0)
    decay_b = jnp.where(d <= 0, jnp.exp(jnp.maximum(-d, 0.0) * lgb), 0.0)
    ri = lax.broadcasted_iota(jnp.int32, (c, dk), 0).astype(f32)
    qdec_f = jnp.exp((ri + 1.0) * lgf)
    kdec_f = jnp.exp((c - 1.0 - ri) * lgf)
    qdec_b = jnp.exp((c - ri) * lgb)
    kdec_b = jnp.exp(ri * lgb)
    cd_f = jnp.exp(jnp.full((dk, dk), c * lgf, f32))
    cd_b = jnp.exp(jnp.full((dk, dk), c * lgb, f32))
    kscale = dk ** -0.5

    def chunk(n, state, decay, qdec, kdec, cd, o_scr):
        r = pl.ds(pl.multiple_of(n * c, c), c)
        q = q_ref[r, :]
        k = k_ref[r, :] * kscale
        vb = v_ref[r, :].astype(bf16)
        scores = _dot_nt(q.astype(bf16), k.astype(bf16)) * decay
        o = _dot(scores.astype(bf16), vb) + _dot((q * qdec).astype(bf16), state.astype(bf16))
        o_scr[r, :] = o
        return cd * state + _dot((k * kdec).T.astype(bf16), vb)

    sf = lax.fori_loop(0, n_chunks, lambda n, s: chunk(n, s, decay_f, qdec_f, kdec_f, cd_f, of_scr), s0f_ref[0, 0])
    sf_ref[0, 0] = sf
    sb = lax.fori_loop(0, n_chunks,
                       lambda n, s: chunk(n_chunks - 1 - n, s, decay_b, qdec_b, kdec_b, cd_b, ob_scr), s0b_ref[0, 0])
    sb_ref[0, 0] = sb

    def finish(n, carry):
        r = pl.ds(pl.multiple_of(n * c, c), c)

        def hn(o):
            return o * lax.rsqrt(jnp.mean(o * o, axis=-1, keepdims=True) + NORM_EPS)

        gf = gf_ref[r, :]
        gb = gb_ref[r, :]
        out = hn(of_scr[r, :]) * (gf * jax.nn.sigmoid(gf)) + hn(ob_scr[r, :]) * (gb * jax.nn.sigmoid(gb))
        o_ref[r, :] = out.astype(o_ref.dtype)
        return carry

    lax.fori_loop(0, n_chunks, finish, 0)


def retention(proj, log_gamma, s0f, s0b, o_ret, row0, nb, seq, col0, n_heads, dk):
    n_chunks = seq // RET_CHUNK
    r0 = row0 // seq
    c0 = col0 // dk

    def col_spec(k):
        return pl.BlockSpec((seq, dk), lambda b, h: (r0 + b, c0 + k * n_heads + h))

    st_spec = pl.BlockSpec((1, 1, dk, dk), lambda b, h: (b, h, 0, 0))
    st_shape = jax.ShapeDtypeStruct((nb, n_heads, dk, dk), f32)
    return pl.pallas_call(
        functools.partial(_ret_kernel, n_chunks=n_chunks, dk=dk),
        grid=(nb, n_heads),
        in_specs=[pl.BlockSpec(memory_space=pltpu.SMEM),
                  col_spec(0), col_spec(1), col_spec(2), col_spec(3), col_spec(4),
                  st_spec, st_spec, pl.BlockSpec(memory_space=pl.ANY)],
        out_specs=[pl.BlockSpec((seq, dk), lambda b, h: (r0 + b, h)), st_spec, st_spec],
        out_shape=[jax.ShapeDtypeStruct(o_ret.shape, o_ret.dtype), st_shape, st_shape],
        scratch_shapes=[pltpu.VMEM((seq, dk), f32), pltpu.VMEM((seq, dk), f32)],
        input_output_aliases={8: 0},
        compiler_params=_params("parallel", "parallel"),
        name="retention",
    )(log_gamma, proj, proj, proj, proj, proj, s0f, s0b, o_ret)


def _merge_kernel(oa_ref, or_ref, wa_ref, wr_ref, ga_ref, gr_ref, o_ref):
    a = _dot(oa_ref[...], wa_ref[...])
    r = _dot(or_ref[...], wr_ref[...])
    o_ref[...] = (jax.nn.sigmoid(ga_ref[...]) * a + jax.nn.sigmoid(gr_ref[...]) * r).astype(o_ref.dtype)


def merge_branches(o_att, o_ret, w_pa, w_pr, proj, gate_col0, tm, tn):
    ntok, qw = o_att.shape
    rw = o_ret.shape[1]
    d = w_pa.shape[1]
    ga0 = gate_col0 // tn
    gr0 = (gate_col0 + d) // tn
    return pl.pallas_call(
        _merge_kernel,
        grid=(ntok // tm, d // tn),
        in_specs=[pl.BlockSpec((tm, qw), lambda i, j: (i, 0)),
                  pl.BlockSpec((tm, rw), lambda i, j: (i, 0)),
                  pl.BlockSpec((qw, tn), lambda i, j: (0, j)),
                  pl.BlockSpec((rw, tn), lambda i, j: (0, j)),
                  pl.BlockSpec((tm, tn), lambda i, j: (i, ga0 + j)),
                  pl.BlockSpec((tm, tn), lambda i, j: (i, gr0 + j))],
        out_specs=pl.BlockSpec((tm, tn), lambda i, j: (i, j)),
        out_shape=jax.ShapeDtypeStruct((ntok, d), bf16),
        compiler_params=_params("parallel", "arbitrary"),
        name="merge_branches",
    )(o_att, o_ret, w_pa, w_pr, proj, proj)


def _out_proj_kernel(m_ref, w_ref, x_ref, g_ref, o_ref):
    o_ref[...] = x_ref[...] + g_ref[0] * _dot(m_ref[...], w_ref[...])


def out_projection(merged, w_out, x, mod3, row_of_tile, k_gate, tm, tn):
    ntok, d = x.shape
    kd = merged.shape[1]
    return pl.pallas_call(
        _out_proj_kernel,
        grid=(ntok // tm, d // tn),
        in_specs=[pl.BlockSpec((tm, kd), lambda i, j: (i, 0)),
                  pl.BlockSpec((kd, tn), lambda i, j: (0, j)),
                  pl.BlockSpec((tm, tn), lambda i, j: (i, j)),
                  pl.BlockSpec((1, 1, tn), lambda i, j: (row_of_tile(i, tm) * 6 + k_gate, 0, j))],
        out_specs=pl.BlockSpec((tm, tn), lambda i, j: (i, j)),
        out_shape=jax.ShapeDtypeStruct((ntok, d), f32),
        compiler_params=_params("parallel", "arbitrary"),
        name="out_projection",
    )(merged, w_out, x, mod3)


def _extract_topk(x, k):
    rows = x.shape[0]
    iot = lax.broadcasted_iota(jnp.int32, x.shape, 0).astype(f32)
    kiota = lax.broadcasted_iota(jnp.int32, (k, x.shape[1]), 0)

    def body(r, carry):
        x, rank, vals = carry
        m = jnp.max(x, axis=0, keepdims=True)
        idx = jnp.min(jnp.where(x == m, iot, float(rows)), axis=0, keepdims=True)
        sel = iot == idx
        rank = jnp.where(sel, r.astype(f32), rank)
        x = jnp.where(sel, -jnp.inf, x)
        vals = jnp.where(kiota == r, m, vals)
        return x, rank, vals

    init = (x, jnp.full(x.shape, float(k), f32), jnp.zeros((k, x.shape[1]), f32))
    _, rank, vals = lax.fori_loop(0, k, body, init)
    return rank, vals


def _peer_route_kernel(q_ref, keys_ref, n1_ref, e1_ref, r2_ref, e2_ref, *, half):
    k = PEER_TOPK
    q1 = q_ref[:, :half].astype(bf16)
    q2 = q_ref[:, half:].astype(bf16)
    s1 = _dot_nt(keys_ref[0, 0].astype(bf16), q1)
    s2 = _dot_nt(keys_ref[0, 1].astype(bf16), q2)
    rank1, v1 = _extract_topk(s1, k)
    rank2, v2 = _extract_topk(s2, k)
    cand = jnp.concatenate([v1[a:a + 1, :] + v2 for a in range(k)], axis=0)
    crank, cvals = _extract_topk(cand, k)
    cnt = [jnp.sum(jnp.where(crank[a * k:(a + 1) * k, :] < float(k), 1.0, 0.0), axis=0, keepdims=True)
           for a in range(k)]
    z = jnp.sum(jnp.exp(cvals - cvals[0:1, :]), axis=0, keepdims=True)
    n1 = jnp.zeros_like(rank1)
    for a in range(k):
        n1 = jnp.where(rank1 == float(a), cnt[a], n1)
    n1_ref[0] = n1
    e1_ref[0] = jnp.exp(s1 - v1[0:1, :])
    r2_ref[0] = rank2
    e2_ref[0] = jnp.exp(s2 - v2[0:1, :]) / z


def peer_route(qp, sub_keys, t_tile):
    ntok = qp.shape[0]
    heads, _, n_keys, half = sub_keys.shape
    out_spec = pl.BlockSpec((1, n_keys, t_tile), lambda i, h: (h, 0, i))
    out_shape = jax.ShapeDtypeStruct((heads, n_keys, ntok), f32)
    return pl.pallas_call(
        functools.partial(_peer_route_kernel, half=half),
        grid=(ntok // t_tile, heads),
        in_specs=[pl.BlockSpec((t_tile, 2 * half), lambda i, h: (i, h)),
                  pl.BlockSpec((1, 2, n_keys, half), lambda i, h: (h, 0, 0, 0))],
        out_specs=[out_spec] * 4,
        out_shape=[out_shape] * 4,
        compiler_params=_params("parallel", "parallel"),
        name="peer_route",
    )(qp, sub_keys)


def _peer_dense_kernel(ht_ref, u_ref, vt_ref, n1_ref, e1_ref, r2_ref, e2_ref, acc_ref, coef_ref,
                       *, heads, n_keys):
    e = pl.program_id(1)
    et = u_ref.shape[0]

    @pl.when(e == 0)
    def _():
        acc_ref[...] = jnp.zeros_like(acc_ref)

    act = _dot(u_ref[...], ht_ref[...])
    for c in range(et // n_keys):
        i = e * (et // n_keys) + c
        gate = None
        for h in range(heads):
            n1 = n1_ref[h, pl.ds(i, 1), :]
            e1 = e1_ref[h, pl.ds(i, 1), :]
            term = jnp.where(r2_ref[h] < n1, e2_ref[h] * e1, 0.0)
            gate = term if gate is None else gate + term
        a = act[c * n_keys:(c + 1) * n_keys, :]
        gelu = 0.5 * a * (1.0 + lax.erf(a * (2.0 ** -0.5)))
        coef_ref[c * n_keys:(c + 1) * n_keys, :] = (gate * gelu).astype(bf16)
    acc_ref[...] += _dot(vt_ref[...], coef_ref[...])


def peer_dense(h_t, u, v_t, n1, e1, r2, e2, t_tile, e_tile):
    d, ntok = h_t.shape
    n_exp = u.shape[0]
    heads, n_keys, _ = n1.shape
    once = pl.Buffered(1)
    route_spec = pl.BlockSpec((heads, n_keys, t_tile), lambda t, e: (0, 0, t), pipeline_mode=once)
    return pl.pallas_call(
        functools.partial(_peer_dense_kernel, heads=heads, n_keys=n_keys),
        grid=(ntok // t_tile, n_exp // e_tile),
        in_specs=[pl.BlockSpec((d, t_tile), lambda t, e: (0, t), pipeline_mode=once),
                  pl.BlockSpec((e_tile, d), lambda t, e: (e, 0)),
                  pl.BlockSpec((d, e_tile), lambda t, e: (0, e)),
                  route_spec, route_spec, route_spec, route_spec],
        out_specs=pl.BlockSpec((d, t_tile), lambda t, e: (0, t)),
        out_shape=jax.ShapeDtypeStruct((d, ntok), f32),
        scratch_shapes=[pltpu.VMEM((e_tile, t_tile), bf16)],
        compiler_params=_params("parallel", "arbitrary"),
        name="peer_dense",
    )(h_t, u, v_t, n1, e1, r2, e2)


def _residual_kernel(x_ref, p_ref, g_ref, o_ref):
    o_ref[...] = x_ref[...] + g_ref[0] * p_ref[...]


def _residual_norm_kernel(x_ref, p_ref, g_ref, w_ref, o_ref):
    x = x_ref[...] + g_ref[0] * p_ref[...]
    y = x * lax.rsqrt(jnp.mean(x * x, axis=-1, keepdims=True) + NORM_EPS)
    o_ref[...] = y * w_ref[...]


def gated_residual(x, p, mod3, row_of_tile, k_gate, tm, norm_w=None):
    ntok, d = x.shape
    row_spec = pl.BlockSpec((tm, d), lambda i: (i, 0))
    in_specs = [row_spec, row_spec, pl.BlockSpec((1, 1, d), lambda i: (row_of_tile(i, tm) * 6 + k_gate, 0, 0))]
    args = [x, p, mod3]
    kern = _residual_kernel
    if norm_w is not None:
        in_specs.append(pl.BlockSpec((1, d), lambda i: (0, 0)))
        args.append(norm_w.reshape(1, d))
        kern = _residual_norm_kernel
    return pl.pallas_call(
        kern,
        grid=(ntok // tm,),
        in_specs=in_specs,
        out_specs=row_spec,
        out_shape=jax.ShapeDtypeStruct((ntok, d), f32),
        compiler_params=_params("parallel"),
        name="gated_residual",
    )(*args)


def _axial_rope_tables(n_tokens, hd):
    rows = n_tokens // GRID_W
    row = jnp.repeat(jnp.arange(rows), GRID_W).astype(f32)
    col = jnp.tile(jnp.arange(GRID_W), rows).astype(f32)
    half = hd // 2
    inv = 1.0 / (ROPE_BASE ** (jnp.arange(0, half, 2, dtype=f32) / half))
    ar = row[:, None] * inv[None, :]
    ac = col[:, None] * inv[None, :]
    ang = jnp.concatenate([ar, ar, ac, ac], axis=-1)
    return jnp.cos(ang), jnp.sin(ang)


def kernel(x_prompt, x_sample, cache_attn_k, cache_attn_v, state_ret_fwd, state_ret_bwd, c, c_ctx, w_ada, b_ada,
           norm_mix, norm_ffn, w_in, attn_sink, ret_decay_logit, w_proj_attn, w_proj_ret, w_out, peer_w_query,
           peer_sub_keys, peer_u, peer_v, norm_final):
    batch, seq, d = x_prompt.shape
    dbatch, dseq, _ = x_sample.shape
    depth = w_in.shape[0]
    assert depth >= 1
    n_heads = attn_sink.shape[1]
    _, _, past, n_kv, hd = cache_attn_k.shape
    ret_heads, ret_dk, ret_dv = state_ret_fwd.shape[2:]
    assert ret_dk == ret_dv
    peer_heads, peer_dkey = peer_w_query.shape[2:]
    q_w = n_heads * hd
    kv_w = n_kv * hd
    r_w = ret_heads * ret_dk
    ret_col0 = q_w + 2 * kv_w
    gate_col0 = ret_col0 + 5 * r_w

    n_prompt = batch * seq
    n_lat = dbatch * dseq
    ntok = n_prompt + n_lat

    def row_of_tile(i, tm):
        return jnp.where(i < n_prompt // tm, 0, 1 + (i - n_prompt // tm) // (dseq // tm))

    x = jnp.concatenate([x_prompt.reshape(n_prompt, d), x_sample.reshape(n_lat, d)], axis=0)
    n_cond = -(-(dbatch + 1) // 8) * 8
    cond = jnp.zeros((n_cond, d), f32).at[0].set(c_ctx).at[1:dbatch + 1].set(c)
    cos, sin = _axial_rope_tables(dseq, hd)
    assert n_prompt % dseq == 0
    tm_small = _tile(512, n_prompt, dseq)
    tm_big = _tile(1024, n_prompt, dseq)
    tn_in = _tile(1024, w_in.shape[2])
    tn_gate = _tile(512, d, gate_col0)
    tn_q = _tile(1024, peer_heads * peer_dkey)

    new_k, new_v, new_sf, new_sb = [], [], [], []
    for l in range(depth):
        mod3 = ada_modulation(cond, w_ada[l], b_ada[l]).reshape(n_cond * 6, 1, d)
        sink = attn_sink[l].astype(f32)
        log_gamma = jax.nn.log_sigmoid(ret_decay_logit[l].astype(f32))

        proj = norm_mod_matmul(x, norm_mix[l], mod3, row_of_tile, 0, 1, w_in[l].astype(bf16), tm_small, tn_in,
                               False)

        o_att = jnp.zeros((ntok, q_w), bf16)
        o_att = context_attention(proj, sink, o_att, batch, seq, n_heads, n_kv, hd)
        o_att = latent_attention(proj, sink, cache_attn_k[:, l].reshape(dbatch, past, kv_w),
                                 cache_attn_v[:, l].reshape(dbatch, past, kv_w), cos, sin, o_att,
                                 n_prompt, dbatch, dseq, n_heads, n_kv, hd)

        o_ret = jnp.zeros((ntok, r_w), bf16)
        zero_state = jnp.zeros((batch, ret_heads, ret_dk, ret_dv), f32)
        o_ret, s_f, s_b = retention(proj, log_gamma, zero_state, zero_state, o_ret, 0, batch, seq,
                                    ret_col0, ret_heads, ret_dk)
        o_ret, _, _ = retention(proj, log_gamma, state_ret_fwd[:, l], state_ret_bwd[:, l], o_ret, n_prompt,
                                dbatch, dseq, ret_col0, ret_heads, ret_dk)

        merged = merge_branches(o_att, o_ret, w_proj_attn[l].astype(bf16), w_proj_ret[l].astype(bf16), proj,
                                gate_col0, tm_big, tn_gate)
        x = out_projection(merged, w_out[l].astype(bf16), x, mod3, row_of_tile, 2, tm_big, tn_gate)

        new_k.append(proj[:n_prompt, q_w:q_w + kv_w].reshape(batch, seq, n_kv, hd))
        new_v.append(proj[:n_prompt, q_w + kv_w:q_w + 2 * kv_w].reshape(batch, seq, n_kv, hd))
        new_sf.append(s_f)
        new_sb.append(s_b)

        wq = peer_w_query[l].reshape(d, peer_heads * peer_dkey).astype(bf16)
        qp, h2 = norm_mod_matmul(x, norm_ffn[l], mod3, row_of_tile, 3, 4, wq, tm_small, tn_q, True)
        n1, e1, r2, e2 = peer_route(qp, peer_sub_keys[l], LANES)
        p_t = peer_dense(h2.T, peer_u[l].astype(bf16), peer_v[l].T.astype(bf16), n1, e1, r2, e2,
                         _tile(512, ntok), _tile(512, peer_u.shape[1]))
        x = gated_residual(x, p_t.T, mod3, row_of_tile, 5, _tile(256, n_prompt, dseq),
                           norm_final if l == depth - 1 else None)

    y_prompt = x[:n_prompt].reshape(batch, seq, d)
    y_sample = x[n_prompt:].reshape(dbatch, dseq, d)
    return (y_prompt, y_sample, jnp.stack(new_k, axis=1), jnp.stack(new_v, axis=1),
            jnp.stack(new_sf, axis=1), jnp.stack(new_sb, axis=1))
```

```python
import functools

import jax
import jax.numpy as jnp
from jax import lax
from jax.experimental import pallas as pl
from jax.experimental.pallas import tpu as pltpu

GRID_W = 64
WINDOW = 128
ATT_BLOCK = 128
ROPE_BASE = 10000.0
RET_CHUNK = 128
PEER_TOPK = 16
NORM_EPS = 1e-6

LANES = 128
SUBLANES = 8
MXU_DIM = 256
VMEM_LIMIT_BYTES = 56 * 1024 * 1024
NEG_BIG = -1e30

f32 = jnp.float32
bf16 = jnp.bfloat16


def _params(*sem):
    return pltpu.CompilerParams(dimension_semantics=sem, vmem_limit_bytes=VMEM_LIMIT_BYTES)


def _tile(pref, *dims):
    t = pref
    while any(dim % t for dim in dims):
        t //= 2
    assert t >= LANES
    return t


def _dot(a, b):
    return jnp.dot(a, b, preferred_element_type=f32)


def _dot_nt(a, b):
    return lax.dot_general(a, b, (((1,), (1,)), ((), ())), preferred_element_type=f32)


def _dot_tn(a, b):
    return lax.dot_general(a, b, (((0,), (0,)), ((), ())), preferred_element_type=f32)


def _ada_kernel(cond_ref, w_ref, b_ref, o_ref):
    c = cond_ref[...]
    s = (c * jax.nn.sigmoid(c)).astype(bf16)
    o_ref[...] = _dot(s, w_ref[...].astype(bf16)) + b_ref[...]


def ada_modulation(cond, w_ada, b_ada):
    r, d = cond.shape
    n = w_ada.shape[1]
    tn = _tile(512, n)
    return pl.pallas_call(
        _ada_kernel,
        grid=(n // tn,),
        in_specs=[pl.BlockSpec((r, d), lambda j: (0, 0)),
                  pl.BlockSpec((d, tn), lambda j: (0, j)),
                  pl.BlockSpec((1, tn), lambda j: (0, j))],
        out_specs=pl.BlockSpec((r, tn), lambda j: (0, j)),
        out_shape=jax.ShapeDtypeStruct((r, n), f32),
        compiler_params=_params("parallel"),
        name="ada_modulation",
    )(cond, w_ada, b_ada.reshape(1, n))


def _norm_mod(x, g, sh, sc):
    y = x * lax.rsqrt(jnp.mean(x * x, axis=-1, keepdims=True) + NORM_EPS)
    y = y * g
    return y * (1.0 + sc) + sh


def _norm_mod_matmul_kernel(x_ref, g_ref, sh_ref, sc_ref, w_ref, o_ref, h_scr):
    @pl.when(pl.program_id(1) == 0)
    def _():
        h_scr[...] = _norm_mod(x_ref[...], g_ref[...], sh_ref[0], sc_ref[0]).astype(bf16)

    o_ref[...] = _dot(h_scr[...], w_ref[...])


def _norm_mod_matmul_h_kernel(x_ref, g_ref, sh_ref, sc_ref, w_ref, o_ref, h_ref):
    @pl.when(pl.program_id(1) == 0)
    def _():
        h_ref[...] = _norm_mod(x_ref[...], g_ref[...], sh_ref[0], sc_ref[0]).astype(bf16)

    o_ref[...] = _dot(h_ref[...], w_ref[...])


def norm_mod_matmul(x, g, mod3, row_of_tile, k_shift, k_scale, w, tm, tn, emit_h):
    ntok, d = x.shape
    n = w.shape[1]

    def mod_spec(k):
        return pl.BlockSpec((1, 1, d), lambda i, j: (row_of_tile(i, tm) * 6 + k, 0, 0))

    in_specs = [pl.BlockSpec((tm, d), lambda i, j: (i, 0)),
                pl.BlockSpec((1, d), lambda i, j: (0, 0)),
                mod_spec(k_shift), mod_spec(k_scale),
                pl.BlockSpec((d, tn), lambda i, j: (0, j))]
    o_spec = pl.BlockSpec((tm, tn), lambda i, j: (i, j))
    if emit_h:
        return pl.pallas_call(
            _norm_mod_matmul_h_kernel,
            grid=(ntok // tm, n // tn),
            in_specs=in_specs,
            out_specs=[o_spec, pl.BlockSpec((tm, d), lambda i, j: (i, 0))],
            out_shape=[jax.ShapeDtypeStruct((ntok, n), f32), jax.ShapeDtypeStruct((ntok, d), bf16)],
            compiler_params=_params("parallel", "arbitrary"),
            name="norm_mod_matmul_h",
        )(x, g.reshape(1, d), mod3, mod3, w)
    return pl.pallas_call(
        _norm_mod_matmul_kernel,
        grid=(ntok // tm, n // tn),
        in_specs=in_specs,
        out_specs=o_spec,
        out_shape=jax.ShapeDtypeStruct((ntok, n), f32),
        scratch_shapes=[pltpu.VMEM((tm, d), bf16)],
        compiler_params=_params("parallel", "arbitrary"),
        name="norm_mod_matmul",
    )(x, g.reshape(1, d), mod3, mod3, w)


def _softmax_pv(logits, values, sink):
    m = sink
    for lg in logits:
        m = jnp.maximum(m, jnp.max(lg, axis=-1, keepdims=True))
    den = jnp.exp(sink - m)
    acc = None
    for lg, v in zip(logits, values):
        p = jnp.exp(lg - m)
        den = den + jnp.sum(p, axis=-1, keepdims=True)
        pv = _dot(p.astype(bf16), v)
        acc = pv if acc is None else acc + pv
    return acc / den


def _ctx_attn_kernel(sink_ref, q_ref, k_ref, v_ref, _, o_ref, *, group, hd):
    kvh = pl.program_id(1)
    scale = hd ** -0.5
    k = k_ref[...].astype(bf16)
    v = v_ref[...].astype(bf16)
    for g in range(group):
        q = q_ref[:, g * hd:(g + 1) * hd].astype(bf16)
        lg = _dot_nt(q, k) * scale
        o = _softmax_pv([lg], [v], sink_ref[kvh * group + g])
        o_ref[:, g * hd:(g + 1) * hd] = o.astype(o_ref.dtype)


def context_attention(proj, sink, o_att, nb, seq, n_heads, n_kv, hd):
    group = n_heads // n_kv
    q_w = n_heads * hd
    kcol = q_w // hd
    vcol = (q_w + n_kv * hd) // hd
    return pl.pallas_call(
        functools.partial(_ctx_attn_kernel, group=group, hd=hd),
        grid=(nb, n_kv),
        in_specs=[pl.BlockSpec(memory_space=pltpu.SMEM),
                  pl.BlockSpec((seq, group * hd), lambda b, h: (b, h)),
                  pl.BlockSpec((seq, hd), lambda b, h: (b, kcol + h)),
                  pl.BlockSpec((seq, hd), lambda b, h: (b, vcol + h)),
                  pl.BlockSpec(memory_space=pl.ANY)],
        out_specs=pl.BlockSpec((seq, group * hd), lambda b, h: (b, h)),
        out_shape=jax.ShapeDtypeStruct(o_att.shape, o_att.dtype),
        input_output_aliases={4: 0},
        compiler_params=_params("parallel", "parallel"),
        name="context_attention",
    )(sink, proj, proj, proj, o_att)


def _rope(x, cos, sin, first_half):
    rot = jnp.where(first_half, -pltpu.roll(x, 3 * (LANES // 4), 1), pltpu.roll(x, LANES // 4, 1))
    return x * cos + rot * sin


def _lat_attn_kernel(sink_ref, q_ref, kp_ref, kc_ref, kn_ref, vp_ref, vc_ref, vn_ref, kx_ref, vx_ref,
                     cp_ref, cc_ref, cn_ref, sp_ref, sc_ref, sn_ref, _, o_ref, *, group, hd):
    n = pl.program_id(1)
    kvh = pl.program_id(2)
    nblk = pl.num_programs(1)
    scale = hd ** -0.5
    blk = (ATT_BLOCK, hd)
    lane = lax.broadcasted_iota(jnp.int32, blk, 1)
    first_half = (lane % (hd // 2)) < (hd // 4)
    row = lax.broadcasted_iota(jnp.int32, (ATT_BLOCK, ATT_BLOCK), 0)
    col = lax.broadcasted_iota(jnp.int32, (ATT_BLOCK, ATT_BLOCK), 1)
    mask_p = jnp.logical_and(col >= row, n > 0)
    mask_n = jnp.logical_and(col <= row, n < nblk - 1)
    kp = _rope(kp_ref[...], cp_ref[...], sp_ref[...], first_half).astype(bf16)
    kc = _rope(kc_ref[...], cc_ref[...], sc_ref[...], first_half).astype(bf16)
    kn = _rope(kn_ref[...], cn_ref[...], sn_ref[...], first_half).astype(bf16)
    kx = kx_ref[0].astype(bf16)
    values = [vp_ref[...].astype(bf16), vc_ref[...].astype(bf16), vn_ref[...].astype(bf16), vx_ref[0].astype(bf16)]
    for g in range(group):
        q = _rope(q_ref[:, g * hd:(g + 1) * hd], cc_ref[...], sc_ref[...], first_half).astype(bf16)
        lp = jnp.where(mask_p, _dot_nt(q, kp) * scale, NEG_BIG)
        lc = _dot_nt(q, kc) * scale
        ln = jnp.where(mask_n, _dot_nt(q, kn) * scale, NEG_BIG)
        lx = _dot_nt(q, kx) * scale
        o = _softmax_pv([lp, lc, ln, lx], values, sink_ref[kvh * group + g])
        o_ref[:, g * hd:(g + 1) * hd] = o.astype(o_ref.dtype)


def latent_attention(proj, sink, k_ctx, v_ctx, cos, sin, o_att, row0, nb, seq, n_heads, n_kv, hd):
    assert WINDOW == ATT_BLOCK
    group = n_heads // n_kv
    q_w = n_heads * hd
    kcol = q_w // hd
    vcol = (q_w + n_kv * hd) // hd
    nblk = seq // ATT_BLOCK
    past = k_ctx.shape[1]
    r0 = row0 // ATT_BLOCK

    def rb(b, n):
        return r0 + b * nblk + n

    def prev(n):
        return jnp.maximum(n - 1, 0)

    def nxt(n):
        return jnp.minimum(n + 1, nblk - 1)

    def kv_spec(col0, f):
        return pl.BlockSpec((ATT_BLOCK, hd), lambda b, n, h: (rb(b, f(n)), col0 + h))

    def tab_spec(f):
        return pl.BlockSpec((ATT_BLOCK, hd), lambda b, n, h: (f(n), 0))

    ident = lambda n: n
    ctx_spec = pl.BlockSpec((1, past, hd), lambda b, n, h: (b, 0, h))
    return pl.pallas_call(
        functools.partial(_lat_attn_kernel, group=group, hd=hd),
        grid=(nb, nblk, n_kv),
        in_specs=[pl.BlockSpec(memory_space=pltpu.SMEM),
                  pl.BlockSpec((ATT_BLOCK, group * hd), lambda b, n, h: (rb(b, n), h)),
                  kv_spec(kcol, prev), kv_spec(kcol, ident), kv_spec(kcol, nxt),
                  kv_spec(vcol, prev), kv_spec(vcol, ident), kv_spec(vcol, nxt),
                  ctx_spec, ctx_spec,
                  tab_spec(prev), tab_spec(ident), tab_spec(nxt),
                  tab_spec(prev), tab_spec(ident), tab_spec(nxt),
                  pl.BlockSpec(memory_space=pl.ANY)],
        out_specs=pl.BlockSpec((ATT_BLOCK, group * hd), lambda b, n, h: (rb(b, n), h)),
        out_shape=jax.ShapeDtypeStruct(o_att.shape, o_att.dtype),
        input_output_aliases={16: 0},
        compiler_params=_params("parallel", "parallel", "parallel"),
        name="latent_attention",
    )(sink, proj, proj, proj, proj, proj, proj, proj, k_ctx, v_ctx, cos, cos, cos, sin, sin, sin, o_att)


RET_HEADS_PER_STEP = 2


def _ret_kernel(lg_ref, q_ref, k_ref, v_ref, gf_ref, gb_ref, s0f_ref, s0b_ref, _, o_ref, sf_ref, sb_ref,
                of_scr, ob_scr, *, n_chunks, dk, hps):
    hb = pl.program_id(1)
    c = RET_CHUNK
    ii = lax.broadcasted_iota(jnp.int32, (c, c), 0).astype(f32)
    jj = lax.broadcasted_iota(jnp.int32, (c, c), 1).astype(f32)
    d = ii - jj
    ri = lax.broadcasted_iota(jnp.int32, (c, dk), 0).astype(f32)
    kscale = dk ** -0.5

    consts = []
    for hh in range(hps):
        lgf = lg_ref[0, hb * hps + hh]
        lgb = lg_ref[1, hb * hps + hh]
        fwd = (jnp.where(d >= 0, jnp.exp(jnp.maximum(d, 0.0) * lgf), 0.0),
               jnp.exp((ri + 1.0) * lgf), jnp.exp((c - 1.0 - ri) * lgf), jnp.exp(jnp.full((dk, dk), c * lgf, f32)))
        bwd = (jnp.where(d <= 0, jnp.exp(jnp.maximum(-d, 0.0) * lgb), 0.0),
               jnp.exp((c - ri) * lgb), jnp.exp(ri * lgb), jnp.exp(jnp.full((dk, dk), c * lgb, f32)))
        consts.append((fwd, bwd))

    def chunk(n, hh, state, cst, o_scr):
        decay, qdec, kdec, cd = cst
        r = pl.ds(pl.multiple_of(n * c, c), c)
        cs = slice(hh * dk, (hh + 1) * dk)
        q = q_ref[r, cs]
        k = k_ref[r, cs] * kscale
        vb = v_ref[r, cs].astype(bf16)
        scores = _dot_nt(q.astype(bf16), k.astype(bf16)) * decay
        o = _dot(scores.astype(bf16), vb) + _dot((q * qdec).astype(bf16), state.astype(bf16))
        o_scr[r, cs] = o
        return cd * state + _dot((k * kdec).T.astype(bf16), vb)

    def body(n, states):
        new = []
        for hh in range(hps):
            sf, sb = states[hh]
            new.append((chunk(n, hh, sf, consts[hh][0], of_scr),
                        chunk(n_chunks - 1 - n, hh, sb, consts[hh][1], ob_scr)))
        return tuple(new)

    init = tuple((s0f_ref[0, hh], s0b_ref[0, hh]) for hh in range(hps))
    final = lax.fori_loop(0, n_chunks, body, init)
    for hh in range(hps):
        sf_ref[0, hh] = final[hh][0]
        sb_ref[0, hh] = final[hh][1]

    def hn(o):
        return o * lax.rsqrt(jnp.mean(o * o, axis=-1, keepdims=True) + NORM_EPS)

    def finish(n, carry):
        r = pl.ds(pl.multiple_of(n * c, c), c)
        for hh in range(hps):
            cs = slice(hh * dk, (hh + 1) * dk)
            gf = gf_ref[r, cs]
            gb = gb_ref[r, cs]
            out = hn(of_scr[r, cs]) * (gf * jax.nn.sigmoid(gf)) + hn(ob_scr[r, cs]) * (gb * jax.nn.sigmoid(gb))
            o_ref[r, cs] = out.astype(o_ref.dtype)
        return carry

    lax.fori_loop(0, n_chunks, finish, 0)


def retention(proj, log_gamma, s0f, s0b, o_ret, row0, nb, seq, col0, n_heads, dk):
    n_chunks = seq // RET_CHUNK
    hps = RET_HEADS_PER_STEP if n_heads % RET_HEADS_PER_STEP == 0 else 1
    w = hps * dk
    r0 = row0 // seq
    c0 = col0 // w
    gpb = n_heads // hps

    def col_spec(k):
        return pl.BlockSpec((seq, w), lambda b, h: (r0 + b, c0 + k * gpb + h))

    st_spec = pl.BlockSpec((1, hps, dk, dk), lambda b, h: (b, h, 0, 0))
    st_shape = jax.ShapeDtypeStruct((nb, n_heads, dk, dk), f32)
    return pl.pallas_call(
        functools.partial(_ret_kernel, n_chunks=n_chunks, dk=dk, hps=hps),
        grid=(nb, gpb),
        in_specs=[pl.BlockSpec(memory_space=pltpu.SMEM),
                  col_spec(0), col_spec(1), col_spec(2), col_spec(3), col_spec(4),
                  st_spec, st_spec, pl.BlockSpec(memory_space=pl.ANY)],
        out_specs=[pl.BlockSpec((seq, w), lambda b, h: (r0 + b, h)), st_spec, st_spec],
        out_shape=[jax.ShapeDtypeStruct(o_ret.shape, o_ret.dtype), st_shape, st_shape],
        scratch_shapes=[pltpu.VMEM((seq, w), f32), pltpu.VMEM((seq, w), f32)],
        input_output_aliases={8: 0},
        compiler_params=_params("parallel", "parallel"),
        name="retention",
    )(log_gamma, proj, proj, proj, proj, proj, s0f, s0b, o_ret)


def _merge_kernel(oa_ref, or_ref, wa_ref, wr_ref, ga_ref, gr_ref, o_ref):
    a = _dot(oa_ref[...], wa_ref[...])
    r = _dot(or_ref[...], wr_ref[...])
    o_ref[...] = (jax.nn.sigmoid(ga_ref[...]) * a + jax.nn.sigmoid(gr_ref[...]) * r).astype(o_ref.dtype)


def merge_branches(o_att, o_ret, w_pa, w_pr, proj, gate_col0, tm, tn):
    ntok, qw = o_att.shape
    rw = o_ret.shape[1]
    d = w_pa.shape[1]
    ga0 = gate_col0 // tn
    gr0 = (gate_col0 + d) // tn
    return pl.pallas_call(
        _merge_kernel,
        grid=(ntok // tm, d // tn),
        in_specs=[pl.BlockSpec((tm, qw), lambda i, j: (i, 0)),
                  pl.BlockSpec((tm, rw), lambda i, j: (i, 0)),
                  pl.BlockSpec((qw, tn), lambda i, j: (0, j)),
                  pl.BlockSpec((rw, tn), lambda i, j: (0, j)),
                  pl.BlockSpec((tm, tn), lambda i, j: (i, ga0 + j)),
                  pl.BlockSpec((tm, tn), lambda i, j: (i, gr0 + j))],
        out_specs=pl.BlockSpec((tm, tn), lambda i, j: (i, j)),
        out_shape=jax.ShapeDtypeStruct((ntok, d), bf16),
        compiler_params=_params("parallel", "arbitrary"),
        name="merge_branches",
    )(o_att, o_ret, w_pa, w_pr, proj, proj)


def _out_proj_kernel(m_ref, w_ref, x_ref, g_ref, o_ref):
    o_ref[...] = x_ref[...] + g_ref[0] * _dot(m_ref[...], w_ref[...])


def out_projection(merged, w_out, x, mod3, row_of_tile, k_gate, tm, tn):
    ntok, d = x.shape
    kd = merged.shape[1]
    return pl.pallas_call(
        _out_proj_kernel,
        grid=(ntok // tm, d // tn),
        in_specs=[pl.BlockSpec((tm, kd), lambda i, j: (i, 0)),
                  pl.BlockSpec((kd, tn), lambda i, j: (0, j)),
                  pl.BlockSpec((tm, tn), lambda i, j: (i, j)),
                  pl.BlockSpec((1, 1, tn), lambda i, j: (row_of_tile(i, tm) * 6 + k_gate, 0, j))],
        out_specs=pl.BlockSpec((tm, tn), lambda i, j: (i, j)),
        out_shape=jax.ShapeDtypeStruct((ntok, d), f32),
        compiler_params=_params("parallel", "arbitrary"),
        name="out_projection",
    )(merged, w_out, x, mod3)


def _extract_round(r, x_ref, rank_ref, vals_ref, pos, sentinel):
    x = x_ref[...]
    m = jnp.max(x, axis=0, keepdims=True)
    idx = jnp.min(jnp.where(x == m, pos, sentinel), axis=0, keepdims=True)
    sel = pos == idx
    rank_ref[...] = jnp.where(sel, r.astype(f32), rank_ref[...])
    x_ref[...] = jnp.where(sel, -jnp.inf, x)
    vals_ref[pl.ds(r, 1), :] = m


def _candidate_layout(k):
    slabs = []
    for a in range(k // 2):
        n_valid = k // (a + 1)
        slabs.append((a, n_valid, -(-n_valid // SUBLANES) * SUBLANES))
    return slabs


def _peer_route_kernel(q_ref, keys_ref, n1_ref, e1_ref, r2_ref, e2_ref, x1_scr, x2_scr, v1_scr, v2_scr,
                       c_scr, crank_scr, cv_scr, *, half):
    k = PEER_TOPK
    n_keys, t = x1_scr.shape
    s1 = _dot_nt(keys_ref[0, 0].astype(bf16), q_ref[:, :half].astype(bf16))
    s2 = _dot_nt(keys_ref[0, 1].astype(bf16), q_ref[:, half:].astype(bf16))
    x1_scr[...] = s1
    x2_scr[...] = s2
    e1_ref[0] = jnp.exp(s1 - jnp.max(s1, axis=0, keepdims=True))
    e2_ref[0] = jnp.exp(s2 - jnp.max(s2, axis=0, keepdims=True))
    n1_ref[0] = jnp.full((n_keys, t), float(k), f32)
    r2_ref[0] = jnp.full((n_keys, t), float(k), f32)
    key_pos = lax.broadcasted_iota(jnp.int32, (n_keys, t), 0).astype(f32)

    def sub_round(r, carry):
        _extract_round(r, x1_scr, n1_ref.at[0], v1_scr, key_pos, float(n_keys))
        _extract_round(r, x2_scr, r2_ref.at[0], v2_scr, key_pos, float(n_keys))
        return carry

    lax.fori_loop(0, k, sub_round, 0)

    v1 = v1_scr[...]
    v2 = v2_scr[...]
    slabs = _candidate_layout(k)
    cands, poss = [], []
    for a, n_valid, n_rows in slabs:
        b = lax.broadcasted_iota(jnp.int32, (n_rows, t), 0)
        cands.append(jnp.where(b < n_valid, v1[a:a + 1, :] + v2[:n_rows, :], -jnp.inf))
        poss.append((a * k + b).astype(f32))
    a_hi = lax.broadcasted_iota(jnp.int32, (k // 2, t), 0) + k // 2
    cands.append(v1[k // 2:, :] + v2[0:1, :])
    poss.append((a_hi * k).astype(f32))
    c_scr[...] = jnp.concatenate(cands, axis=0)
    cand_pos = jnp.concatenate(poss, axis=0)
    crank_scr[...] = jnp.full(c_scr.shape, float(k), f32)

    def cand_round(r, carry):
        _extract_round(r, c_scr, crank_scr, cv_scr, cand_pos, float(k * k))
        return carry

    lax.fori_loop(0, k, cand_round, 0)

    picked = jnp.where(crank_scr[...] < float(k), 1.0, 0.0)
    cnt = []
    row0 = 0
    for a, n_valid, n_rows in slabs:
        cnt.append(jnp.sum(picked[row0:row0 + n_rows, :], axis=0, keepdims=True))
        row0 += n_rows
    for a in range(k // 2, k):
        cnt.append(picked[row0 + a - k // 2:row0 + a - k // 2 + 1, :])
    cvals = cv_scr[...]
    z = jnp.sum(jnp.exp(cvals - cvals[0:1, :]), axis=0, keepdims=True)
    rank1 = n1_ref[0]
    n1 = jnp.zeros_like(rank1)
    for a in range(k):
        n1 = jnp.where(rank1 == float(a), cnt[a], n1)
    n1_ref[0] = n1
    e2_ref[0] = e2_ref[0] / z


def peer_route(qp, sub_keys, t_tile):
    ntok = qp.shape[0]
    heads, _, n_keys, half = sub_keys.shape
    k = PEER_TOPK
    n_cand = sum(s[2] for s in _candidate_layout(k)) + k // 2
    out_spec = pl.BlockSpec((1, n_keys, t_tile), lambda i, h: (h, 0, i))
    out_shape = jax.ShapeDtypeStruct((heads, n_keys, ntok), f32)
    return pl.pallas_call(
        functools.partial(_peer_route_kernel, half=half),
        grid=(ntok // t_tile, heads),
        in_specs=[pl.BlockSpec((t_tile, 2 * half), lambda i, h: (i, h)),
                  pl.BlockSpec((1, 2, n_keys, half), lambda i, h: (h, 0, 0, 0))],
        out_specs=[out_spec] * 4,
        out_shape=[out_shape] * 4,
        scratch_shapes=[pltpu.VMEM((n_keys, t_tile), f32), pltpu.VMEM((n_keys, t_tile), f32),
                        pltpu.VMEM((k, t_tile), f32), pltpu.VMEM((k, t_tile), f32),
                        pltpu.VMEM((n_cand, t_tile), f32), pltpu.VMEM((n_cand, t_tile), f32),
                        pltpu.VMEM((k, t_tile), f32)],
        compiler_params=_params("parallel", "parallel"),
        name="peer_route",
    )(qp, sub_keys)


def _peer_dense_kernel(ht_ref, u_ref, v_ref, n1_ref, e1_ref, r2_ref, e2_ref, acc_ref, act_a, act_b, coef_scr,
                       n1_scr, e1_scr, *, heads, n_keys, n_et):
    g = pl.program_id(0)
    et = u_ref.shape[0]
    e_prev = jnp.maximum(g - 1, 0) % n_et

    @pl.when(g == 0)
    def _():
        act_b[...] = jnp.zeros_like(act_b)

    @pl.when(e_prev == 0)
    def _():
        acc_ref[...] = jnp.zeros_like(acc_ref)

    def step(act_next, act_prev):
        cpt = et // n_keys
        for h in range(heads):
            for c in range(cpt):
                i = e_prev * cpt + c
                n1_scr[h * cpt + c:h * cpt + c + 1, :] = n1_ref[h, pl.ds(i, 1), :]
                e1_scr[h * cpt + c:h * cpt + c + 1, :] = e1_ref[h, pl.ds(i, 1), :]
        act_next[...] = _dot(u_ref[...], ht_ref[...])
        for c in range(cpt):
            rows = slice(c * n_keys, (c + 1) * n_keys)
            n1_rows = [n1_scr[h * cpt + c:h * cpt + c + 1, :] for h in range(heads)]
            e1_rows = [e1_scr[h * cpt + c:h * cpt + c + 1, :] for h in range(heads)]
            for tk in range(ht_ref.shape[1] // LANES):
                cols = slice(tk * LANES, (tk + 1) * LANES)
                gate = None
                for h in range(heads):
                    n1 = n1_rows[h][:, cols]
                    e1 = e1_rows[h][:, cols]
                    term = jnp.where(r2_ref[h, :, cols] < n1, e2_ref[h, :, cols] * e1, 0.0)
                    gate = term if gate is None else gate + term
                a = act_prev[rows, cols]
                gelu = 0.5 * a * (1.0 + lax.erf(a * (2.0 ** -0.5)))
                coef_scr[rows, cols] = (gate * gelu).astype(bf16)
        acc_ref[...] += _dot_tn(coef_scr[...], v_ref[...])

    @pl.when(g % 2 == 0)
    def _():
        step(act_a, act_b)

    @pl.when(g % 2 == 1)
    def _():
        step(act_b, act_a)


def peer_dense(h_t, u, v, n1, e1, r2, e2, t_tile, e_tile):
    d, ntok = h_t.shape
    n_exp = u.shape[0]
    heads, n_keys, _ = n1.shape
    n_et = n_exp // e_tile
    n_steps = (ntok // t_tile) * n_et

    def cur(g):
        gc = jnp.minimum(g, n_steps - 1)
        return gc // n_et, gc % n_et

    def prev(g):
        gp = jnp.maximum(g - 1, 0)
        return gp // n_et, gp % n_et

    once = pl.Buffered(1)
    route_spec = pl.BlockSpec((heads, n_keys, t_tile), lambda g: (0, 0, prev(g)[0]), pipeline_mode=once)
    cpt = e_tile // n_keys
    return pl.pallas_call(
        functools.partial(_peer_dense_kernel, heads=heads, n_keys=n_keys, n_et=n_et),
        grid=(n_steps + 1,),
        in_specs=[pl.BlockSpec((d, t_tile), lambda g: (0, cur(g)[0]), pipeline_mode=once),
                  pl.BlockSpec((e_tile, d), lambda g: (cur(g)[1], 0)),
                  pl.BlockSpec((e_tile, d), lambda g: (prev(g)[1], 0)),
                  route_spec, route_spec, route_spec, route_spec],
        out_specs=pl.BlockSpec((t_tile, d), lambda g: (prev(g)[0], 0)),
        out_shape=jax.ShapeDtypeStruct((ntok, d), f32),
        scratch_shapes=[pltpu.VMEM((e_tile, t_tile), f32), pltpu.VMEM((e_tile, t_tile), f32),
                        pltpu.VMEM((e_tile, t_tile), bf16),
                        pltpu.VMEM((heads * cpt, t_tile), f32), pltpu.VMEM((heads * cpt, t_tile), f32)],
        compiler_params=_params("arbitrary"),
        name="peer_dense",
    )(h_t, u, v, n1, e1, r2, e2)


def _residual_kernel(x_ref, p_ref, g_ref, o_ref):
    o_ref[...] = x_ref[...] + g_ref[0] * p_ref[...]


def _residual_norm_kernel(x_ref, p_ref, g_ref, w_ref, o_ref):
    x = x_ref[...] + g_ref[0] * p_ref[...]
    y = x * lax.rsqrt(jnp.mean(x * x, axis=-1, keepdims=True) + NORM_EPS)
    o_ref[...] = y * w_ref[...]


def gated_residual(x, p, mod3, row_of_tile, k_gate, tm, norm_w=None):
    ntok, d = x.shape
    row_spec = pl.BlockSpec((tm, d), lambda i: (i, 0))
    in_specs = [row_spec, row_spec, pl.BlockSpec((1, 1, d), lambda i: (row_of_tile(i, tm) * 6 + k_gate, 0, 0))]
    args = [x, p, mod3]
    kern = _residual_kernel
    if norm_w is not None:
        in_specs.append(pl.BlockSpec((1, d), lambda i: (0, 0)))
        args.append(norm_w.reshape(1, d))
        kern = _residual_norm_kernel
    return pl.pallas_call(
        kern,
        grid=(ntok // tm,),
        in_specs=in_specs,
        out_specs=row_spec,
        out_shape=jax.ShapeDtypeStruct((ntok, d), f32),
        compiler_params=_params("parallel"),
        name="gated_residual",
    )(*args)


def _axial_rope_tables(n_tokens, hd):
    rows = n_tokens // GRID_W
    row = jnp.repeat(jnp.arange(rows), GRID_W).astype(f32)
    col = jnp.tile(jnp.arange(GRID_W), rows).astype(f32)
    half = hd // 2
    inv = 1.0 / (ROPE_BASE ** (jnp.arange(0, half, 2, dtype=f32) / half))
    ar = row[:, None] * inv[None, :]
    ac = col[:, None] * inv[None, :]
    ang = jnp.concatenate([ar, ar, ac, ac], axis=-1)
    return jnp.cos(ang), jnp.sin(ang)


def kernel(x_prompt, x_sample, cache_attn_k, cache_attn_v, state_ret_fwd, state_ret_bwd, c, c_ctx, w_ada, b_ada,
           norm_mix, norm_ffn, w_in, attn_sink, ret_decay_logit, w_proj_attn, w_proj_ret, w_out, peer_w_query,
           peer_sub_keys, peer_u, peer_v, norm_final):
    batch, seq, d = x_prompt.shape
    dbatch, dseq, _ = x_sample.shape
    depth = w_in.shape[0]
    assert depth >= 1
    n_heads = attn_sink.shape[1]
    _, _, past, n_kv, hd = cache_attn_k.shape
    ret_heads, ret_dk, ret_dv = state_ret_fwd.shape[2:]
    assert ret_dk == ret_dv
    peer_heads, peer_dkey = peer_w_query.shape[2:]
    q_w = n_heads * hd
    kv_w = n_kv * hd
    r_w = ret_heads * ret_dk
    ret_col0 = q_w + 2 * kv_w
    gate_col0 = ret_col0 + 5 * r_w

    n_prompt = batch * seq
    n_lat = dbatch * dseq
    ntok = n_prompt + n_lat

    def row_of_tile(i, tm):
        return jnp.where(i < n_prompt // tm, 0, 1 + (i - n_prompt // tm) // (dseq // tm))

    x = jnp.concatenate([x_prompt.reshape(n_prompt, d), x_sample.reshape(n_lat, d)], axis=0)
    n_cond = -(-(dbatch + 1) // 8) * 8
    cond = jnp.zeros((n_cond, d), f32).at[0].set(c_ctx).at[1:dbatch + 1].set(c)
    cos, sin = _axial_rope_tables(dseq, hd)
    assert n_prompt % dseq == 0
    tm_small = _tile(512, n_prompt, dseq)
    tm_big = _tile(1024, n_prompt, dseq)
    tn_in = _tile(1024, w_in.shape[2])
    tn_gate = _tile(512, d, gate_col0)
    tn_q = _tile(1024, peer_heads * peer_dkey)

    new_k, new_v, new_sf, new_sb = [], [], [], []
    for l in range(depth):
        mod3 = ada_modulation(cond, w_ada[l], b_ada[l]).reshape(n_cond * 6, 1, d)
        sink = attn_sink[l].astype(f32)
        log_gamma = jax.nn.log_sigmoid(ret_decay_logit[l].astype(f32))

        proj = norm_mod_matmul(x, norm_mix[l], mod3, row_of_tile, 0, 1, w_in[l].astype(bf16), tm_small, tn_in,
                               False)

        o_att = jnp.zeros((ntok, q_w), bf16)
        o_att = context_attention(proj, sink, o_att, batch, seq, n_heads, n_kv, hd)
        o_att = latent_attention(proj, sink, cache_attn_k[:, l].reshape(dbatch, past, kv_w),
                                 cache_attn_v[:, l].reshape(dbatch, past, kv_w), cos, sin, o_att,
                                 n_prompt, dbatch, dseq, n_heads, n_kv, hd)

        o_ret = jnp.zeros((ntok, r_w), bf16)
        zero_state = jnp.zeros((batch, ret_heads, ret_dk, ret_dv), f32)
        o_ret, s_f, s_b = retention(proj, log_gamma, zero_state, zero_state, o_ret, 0, batch, seq,
                                    ret_col0, ret_heads, ret_dk)
        o_ret, _, _ = retention(proj, log_gamma, state_ret_fwd[:, l], state_ret_bwd[:, l], o_ret, n_prompt,
                                dbatch, dseq, ret_col0, ret_heads, ret_dk)

        merged = merge_branches(o_att, o_ret, w_proj_attn[l].astype(bf16), w_proj_ret[l].astype(bf16), proj,
                                gate_col0, tm_big, tn_gate)
        x = out_projection(merged, w_out[l].astype(bf16), x, mod3, row_of_tile, 2, tm_big, tn_gate)

        new_k.append(proj[:n_prompt, q_w:q_w + kv_w].reshape(batch, seq, n_kv, hd))
        new_v.append(proj[:n_prompt, q_w + kv_w:q_w + 2 * kv_w].reshape(batch, seq, n_kv, hd))
        new_sf.append(s_f)
        new_sb.append(s_b)

        wq = peer_w_query[l].reshape(d, peer_heads * peer_dkey).astype(bf16)
        qp, h2 = norm_mod_matmul(x, norm_ffn[l], mod3, row_of_tile, 3, 4, wq, tm_small, tn_q, True)
        n1, e1, r2, e2 = peer_route(qp, peer_sub_keys[l], _tile(512, ntok))
        p = peer_dense(h2.T, peer_u[l].astype(bf16), peer_v[l].astype(bf16), n1, e1, r2, e2,
                       _tile(512, ntok), _tile(512, peer_u.shape[1]))
        x = gated_residual(x, p, mod3, row_of_tile, 5, _tile(256, n_prompt, dseq),
                           norm_final if l == depth - 1 else None)

    y_prompt = x[:n_prompt].reshape(batch, seq, d)
    y_sample = x[n_prompt:].reshape(dbatch, dseq, d)
    return (y_prompt, y_sample, jnp.stack(new_k, axis=1), jnp.stack(new_v, axis=1),
            jnp.stack(new_sf, axis=1), jnp.stack(new_sb, axis=1))
```

```python
import functools

import jax
import jax.numpy as jnp
from jax import lax
from jax.experimental import pallas as pl
from jax.experimental.pallas import tpu as pltpu

GRID_W = 64
WINDOW = 128
ATT_BLOCK = 128
ROPE_BASE = 10000.0
RET_CHUNK = 128
PEER_TOPK = 16
NORM_EPS = 1e-6

LANES = 128
SUBLANES = 8
MXU_DIM = 256
VMEM_LIMIT_BYTES = 56 * 1024 * 1024
NEG_BIG = -1e30

f32 = jnp.float32
bf16 = jnp.bfloat16


def _params(*sem):
    return pltpu.CompilerParams(dimension_semantics=sem, vmem_limit_bytes=VMEM_LIMIT_BYTES)


def _tile(pref, *dims):
    t = pref
    while any(dim % t for dim in dims):
        t //= 2
    assert t >= LANES
    return t


def _dot(a, b):
    return jnp.dot(a, b, preferred_element_type=f32)


def _dot_nt(a, b):
    return lax.dot_general(a, b, (((1,), (1,)), ((), ())), preferred_element_type=f32)


def _dot_tn(a, b):
    return lax.dot_general(a, b, (((0,), (0,)), ((), ())), preferred_element_type=f32)


def _ada_kernel(cond_ref, w_ref, b_ref, o_ref):
    c = cond_ref[...]
    s = (c * jax.nn.sigmoid(c)).astype(bf16)
    o_ref[...] = _dot(s, w_ref[...].astype(bf16)) + b_ref[...]


def ada_modulation(cond, w_ada, b_ada):
    r, d = cond.shape
    n = w_ada.shape[1]
    tn = _tile(512, n)
    return pl.pallas_call(
        _ada_kernel,
        grid=(n // tn,),
        in_specs=[pl.BlockSpec((r, d), lambda j: (0, 0)),
                  pl.BlockSpec((d, tn), lambda j: (0, j)),
                  pl.BlockSpec((1, tn), lambda j: (0, j))],
        out_specs=pl.BlockSpec((r, tn), lambda j: (0, j)),
        out_shape=jax.ShapeDtypeStruct((r, n), f32),
        compiler_params=_params("parallel"),
        name="ada_modulation",
    )(cond, w_ada, b_ada.reshape(1, n))


def _norm_mod(x, g, sh, sc):
    y = x * lax.rsqrt(jnp.mean(x * x, axis=-1, keepdims=True) + NORM_EPS)
    y = y * g
    return y * (1.0 + sc) + sh


def _norm_mod_matmul_kernel(x_ref, g_ref, sh_ref, sc_ref, w_ref, o_ref, h_scr):
    @pl.when(pl.program_id(1) == 0)
    def _():
        h_scr[...] = _norm_mod(x_ref[...], g_ref[...], sh_ref[0], sc_ref[0]).astype(bf16)

    o_ref[...] = _dot(h_scr[...], w_ref[...])


def _norm_mod_matmul_h_kernel(x_ref, g_ref, sh_ref, sc_ref, w_ref, o_ref, h_ref):
    @pl.when(pl.program_id(1) == 0)
    def _():
        h_ref[...] = _norm_mod(x_ref[...], g_ref[...], sh_ref[0], sc_ref[0]).astype(bf16)

    o_ref[...] = _dot(h_ref[...], w_ref[...])


def norm_mod_matmul(x, g, mod3, row_of_tile, k_shift, k_scale, w, tm, tn, emit_h):
    ntok, d = x.shape
    n = w.shape[1]

    def mod_spec(k):
        return pl.BlockSpec((1, 1, d), lambda i, j: (row_of_tile(i, tm) * 6 + k, 0, 0))

    in_specs = [pl.BlockSpec((tm, d), lambda i, j: (i, 0)),
                pl.BlockSpec((1, d), lambda i, j: (0, 0)),
                mod_spec(k_shift), mod_spec(k_scale),
                pl.BlockSpec((d, tn), lambda i, j: (0, j))]
    o_spec = pl.BlockSpec((tm, tn), lambda i, j: (i, j))
    if emit_h:
        return pl.pallas_call(
            _norm_mod_matmul_h_kernel,
            grid=(ntok // tm, n // tn),
            in_specs=in_specs,
            out_specs=[o_spec, pl.BlockSpec((tm, d), lambda i, j: (i, 0))],
            out_shape=[jax.ShapeDtypeStruct((ntok, n), f32), jax.ShapeDtypeStruct((ntok, d), bf16)],
            compiler_params=_params("parallel", "arbitrary"),
            name="norm_mod_matmul_h",
        )(x, g.reshape(1, d), mod3, mod3, w)
    return pl.pallas_call(
        _norm_mod_matmul_kernel,
        grid=(ntok // tm, n // tn),
        in_specs=in_specs,
        out_specs=o_spec,
        out_shape=jax.ShapeDtypeStruct((ntok, n), f32),
        scratch_shapes=[pltpu.VMEM((tm, d), bf16)],
        compiler_params=_params("parallel", "arbitrary"),
        name="norm_mod_matmul",
    )(x, g.reshape(1, d), mod3, mod3, w)


def _softmax_pv(logits, values, sink):
    m = sink
    for lg in logits:
        m = jnp.maximum(m, jnp.max(lg, axis=-1, keepdims=True))
    den = jnp.exp(sink - m)
    acc = None
    for lg, v in zip(logits, values):
        p = jnp.exp(lg - m)
        den = den + jnp.sum(p, axis=-1, keepdims=True)
        pv = _dot(p.astype(bf16), v)
        acc = pv if acc is None else acc + pv
    return acc / den


def _ctx_attn_kernel(sink_ref, q_ref, k_ref, v_ref, _, o_ref, *, group, hd):
    kvh = pl.program_id(1)
    scale = hd ** -0.5
    k = k_ref[...].astype(bf16)
    v = v_ref[...].astype(bf16)
    for g in range(group):
        q = q_ref[:, g * hd:(g + 1) * hd].astype(bf16)
        lg = _dot_nt(q, k) * scale
        o = _softmax_pv([lg], [v], sink_ref[kvh * group + g])
        o_ref[:, g * hd:(g + 1) * hd] = o.astype(o_ref.dtype)


def context_attention(proj, sink, o_att, nb, seq, n_heads, n_kv, hd):
    group = n_heads // n_kv
    q_w = n_heads * hd
    kcol = q_w // hd
    vcol = (q_w + n_kv * hd) // hd
    return pl.pallas_call(
        functools.partial(_ctx_attn_kernel, group=group, hd=hd),
        grid=(nb, n_kv),
        in_specs=[pl.BlockSpec(memory_space=pltpu.SMEM),
                  pl.BlockSpec((seq, group * hd), lambda b, h: (b, h)),
                  pl.BlockSpec((seq, hd), lambda b, h: (b, kcol + h)),
                  pl.BlockSpec((seq, hd), lambda b, h: (b, vcol + h)),
                  pl.BlockSpec(memory_space=pl.ANY)],
        out_specs=pl.BlockSpec((seq, group * hd), lambda b, h: (b, h)),
        out_shape=jax.ShapeDtypeStruct(o_att.shape, o_att.dtype),
        input_output_aliases={4: 0},
        compiler_params=_params("parallel", "parallel"),
        name="context_attention",
    )(sink, proj, proj, proj, o_att)


def _rope(x, cos, sin, first_half):
    rot = jnp.where(first_half, -pltpu.roll(x, 3 * (LANES // 4), 1), pltpu.roll(x, LANES // 4, 1))
    return x * cos + rot * sin


def _lat_attn_kernel(sink_ref, q_ref, kp_ref, kc_ref, kn_ref, vp_ref, vc_ref, vn_ref, kx_ref, vx_ref,
                     cp_ref, cc_ref, cn_ref, sp_ref, sc_ref, sn_ref, _, o_ref, *, group, hd):
    n = pl.program_id(1)
    kvh = pl.program_id(2)
    nblk = pl.num_programs(1)
    scale = hd ** -0.5
    blk = (ATT_BLOCK, hd)
    lane = lax.broadcasted_iota(jnp.int32, blk, 1)
    first_half = (lane % (hd // 2)) < (hd // 4)
    row = lax.broadcasted_iota(jnp.int32, (ATT_BLOCK, ATT_BLOCK), 0)
    col = lax.broadcasted_iota(jnp.int32, (ATT_BLOCK, ATT_BLOCK), 1)
    mask_p = jnp.logical_and(col >= row, n > 0)
    mask_n = jnp.logical_and(col <= row, n < nblk - 1)
    kp = _rope(kp_ref[...], cp_ref[...], sp_ref[...], first_half).astype(bf16)
    kc = _rope(kc_ref[...], cc_ref[...], sc_ref[...], first_half).astype(bf16)
    kn = _rope(kn_ref[...], cn_ref[...], sn_ref[...], first_half).astype(bf16)
    kx = kx_ref[0].astype(bf16)
    values = [vp_ref[...].astype(bf16), vc_ref[...].astype(bf16), vn_ref[...].astype(bf16), vx_ref[0].astype(bf16)]
    for g in range(group):
        q = _rope(q_ref[:, g * hd:(g + 1) * hd], cc_ref[...], sc_ref[...], first_half).astype(bf16)
        lp = jnp.where(mask_p, _dot_nt(q, kp) * scale, NEG_BIG)
        lc = _dot_nt(q, kc) * scale
        ln = jnp.where(mask_n, _dot_nt(q, kn) * scale, NEG_BIG)
        lx = _dot_nt(q, kx) * scale
        o = _softmax_pv([lp, lc, ln, lx], values, sink_ref[kvh * group + g])
        o_ref[:, g * hd:(g + 1) * hd] = o.astype(o_ref.dtype)


def latent_attention(proj, sink, k_ctx, v_ctx, cos, sin, o_att, row0, nb, seq, n_heads, n_kv, hd):
    assert WINDOW == ATT_BLOCK
    group = n_heads // n_kv
    q_w = n_heads * hd
    kcol = q_w // hd
    vcol = (q_w + n_kv * hd) // hd
    nblk = seq // ATT_BLOCK
    past = k_ctx.shape[1]
    r0 = row0 // ATT_BLOCK

    def rb(b, n):
        return r0 + b * nblk + n

    def prev(n):
        return jnp.maximum(n - 1, 0)

    def nxt(n):
        return jnp.minimum(n + 1, nblk - 1)

    def kv_spec(col0, f):
        return pl.BlockSpec((ATT_BLOCK, hd), lambda b, n, h: (rb(b, f(n)), col0 + h))

    def tab_spec(f):
        return pl.BlockSpec((ATT_BLOCK, hd), lambda b, n, h: (f(n), 0))

    ident = lambda n: n
    ctx_spec = pl.BlockSpec((1, past, hd), lambda b, n, h: (b, 0, h))
    return pl.pallas_call(
        functools.partial(_lat_attn_kernel, group=group, hd=hd),
        grid=(nb, nblk, n_kv),
        in_specs=[pl.BlockSpec(memory_space=pltpu.SMEM),
                  pl.BlockSpec((ATT_BLOCK, group * hd), lambda b, n, h: (rb(b, n), h)),
                  kv_spec(kcol, prev), kv_spec(kcol, ident), kv_spec(kcol, nxt),
                  kv_spec(vcol, prev), kv_spec(vcol, ident), kv_spec(vcol, nxt),
                  ctx_spec, ctx_spec,
                  tab_spec(prev), tab_spec(ident), tab_spec(nxt),
                  tab_spec(prev), tab_spec(ident), tab_spec(nxt),
                  pl.BlockSpec(memory_space=pl.ANY)],
        out_specs=pl.BlockSpec((ATT_BLOCK, group * hd), lambda b, n, h: (rb(b, n), h)),
        out_shape=jax.ShapeDtypeStruct(o_att.shape, o_att.dtype),
        input_output_aliases={16: 0},
        compiler_params=_params("parallel", "parallel", "parallel"),
        name="latent_attention",
    )(sink, proj, proj, proj, proj, proj, proj, proj, k_ctx, v_ctx, cos, cos, cos, sin, sin, sin, o_att)


RET_HEADS_PER_STEP = 2


def _ret_kernel(lg_ref, q_ref, k_ref, v_ref, gf_ref, gb_ref, s0f_ref, s0b_ref, _, o_ref, sf_ref, sb_ref,
                of_scr, ob_scr, *, n_chunks, dk, hps):
    hb = pl.program_id(1)
    c = RET_CHUNK
    ii = lax.broadcasted_iota(jnp.int32, (c, c), 0).astype(f32)
    jj = lax.broadcasted_iota(jnp.int32, (c, c), 1).astype(f32)
    d = ii - jj
    ri = lax.broadcasted_iota(jnp.int32, (c, dk), 0).astype(f32)
    kscale = dk ** -0.5

    consts = []
    for hh in range(hps):
        lgf = lg_ref[0, hb * hps + hh]
        lgb = lg_ref[1, hb * hps + hh]
        fwd = (jnp.where(d >= 0, jnp.exp(jnp.maximum(d, 0.0) * lgf), 0.0),
               jnp.exp((ri + 1.0) * lgf), jnp.exp((c - 1.0 - ri) * lgf), jnp.exp(jnp.full((dk, dk), c * lgf, f32)))
        bwd = (jnp.where(d <= 0, jnp.exp(jnp.maximum(-d, 0.0) * lgb), 0.0),
               jnp.exp((c - ri) * lgb), jnp.exp(ri * lgb), jnp.exp(jnp.full((dk, dk), c * lgb, f32)))
        consts.append((fwd, bwd))

    def chunk(n, hh, state, cst, o_scr):
        decay, qdec, kdec, cd = cst
        r = pl.ds(pl.multiple_of(n * c, c), c)
        cs = slice(hh * dk, (hh + 1) * dk)
        q = q_ref[r, cs]
        k = k_ref[r, cs] * kscale
        vb = v_ref[r, cs].astype(bf16)
        scores = _dot_nt(q.astype(bf16), k.astype(bf16)) * decay
        o = _dot(scores.astype(bf16), vb) + _dot((q * qdec).astype(bf16), state.astype(bf16))
        o_scr[r, cs] = o
        return cd * state + _dot((k * kdec).T.astype(bf16), vb)

    def body(n, states):
        new = []
        for hh in range(hps):
            sf, sb = states[hh]
            new.append((chunk(n, hh, sf, consts[hh][0], of_scr),
                        chunk(n_chunks - 1 - n, hh, sb, consts[hh][1], ob_scr)))
        return tuple(new)

    init = tuple((s0f_ref[0, hh], s0b_ref[0, hh]) for hh in range(hps))
    final = lax.fori_loop(0, n_chunks, body, init)
    for hh in range(hps):
        sf_ref[0, hh] = final[hh][0]
        sb_ref[0, hh] = final[hh][1]

    def hn(o):
        return o * lax.rsqrt(jnp.mean(o * o, axis=-1, keepdims=True) + NORM_EPS)

    def finish(n, carry):
        r = pl.ds(pl.multiple_of(n * c, c), c)
        for hh in range(hps):
            cs = slice(hh * dk, (hh + 1) * dk)
            gf = gf_ref[r, cs]
            gb = gb_ref[r, cs]
            out = hn(of_scr[r, cs]) * (gf * jax.nn.sigmoid(gf)) + hn(ob_scr[r, cs]) * (gb * jax.nn.sigmoid(gb))
            o_ref[r, cs] = out.astype(o_ref.dtype)
        return carry

    lax.fori_loop(0, n_chunks, finish, 0)


def retention(proj, log_gamma, s0f, s0b, o_ret, row0, nb, seq, col0, n_heads, dk):
    n_chunks = seq // RET_CHUNK
    hps = RET_HEADS_PER_STEP if n_heads % RET_HEADS_PER_STEP == 0 else 1
    w = hps * dk
    r0 = row0 // seq
    c0 = col0 // w
    gpb = n_heads // hps

    def col_spec(k):
        return pl.BlockSpec((seq, w), lambda b, h: (r0 + b, c0 + k * gpb + h))

    st_spec = pl.BlockSpec((1, hps, dk, dk), lambda b, h: (b, h, 0, 0))
    st_shape = jax.ShapeDtypeStruct((nb, n_heads, dk, dk), f32)
    return pl.pallas_call(
        functools.partial(_ret_kernel, n_chunks=n_chunks, dk=dk, hps=hps),
        grid=(nb, gpb),
        in_specs=[pl.BlockSpec(memory_space=pltpu.SMEM),
                  col_spec(0), col_spec(1), col_spec(2), col_spec(3), col_spec(4),
                  st_spec, st_spec, pl.BlockSpec(memory_space=pl.ANY)],
        out_specs=[pl.BlockSpec((seq, w), lambda b, h: (r0 + b, h)), st_spec, st_spec],
        out_shape=[jax.ShapeDtypeStruct(o_ret.shape, o_ret.dtype), st_shape, st_shape],
        scratch_shapes=[pltpu.VMEM((seq, w), f32), pltpu.VMEM((seq, w), f32)],
        input_output_aliases={8: 0},
        compiler_params=_params("parallel", "parallel"),
        name="retention",
    )(log_gamma, proj, proj, proj, proj, proj, s0f, s0b, o_ret)


def _merge_kernel(oa_ref, or_ref, wa_ref, wr_ref, ga_ref, gr_ref, o_ref):
    a = _dot(oa_ref[...], wa_ref[...])
    r = _dot(or_ref[...], wr_ref[...])
    o_ref[...] = (jax.nn.sigmoid(ga_ref[...]) * a + jax.nn.sigmoid(gr_ref[...]) * r).astype(o_ref.dtype)


def merge_branches(o_att, o_ret, w_pa, w_pr, proj, gate_col0, tm, tn):
    ntok, qw = o_att.shape
    rw = o_ret.shape[1]
    d = w_pa.shape[1]
    ga0 = gate_col0 // tn
    gr0 = (gate_col0 + d) // tn
    return pl.pallas_call(
        _merge_kernel,
        grid=(ntok // tm, d // tn),
        in_specs=[pl.BlockSpec((tm, qw), lambda i, j: (i, 0)),
                  pl.BlockSpec((tm, rw), lambda i, j: (i, 0)),
                  pl.BlockSpec((qw, tn), lambda i, j: (0, j)),
                  pl.BlockSpec((rw, tn), lambda i, j: (0, j)),
                  pl.BlockSpec((tm, tn), lambda i, j: (i, ga0 + j)),
                  pl.BlockSpec((tm, tn), lambda i, j: (i, gr0 + j))],
        out_specs=pl.BlockSpec((tm, tn), lambda i, j: (i, j)),
        out_shape=jax.ShapeDtypeStruct((ntok, d), bf16),
        compiler_params=_params("parallel", "arbitrary"),
        name="merge_branches",
    )(o_att, o_ret, w_pa, w_pr, proj, proj)


def _out_proj_kernel(m_ref, w_ref, x_ref, g_ref, o_ref):
    o_ref[...] = x_ref[...] + g_ref[0] * _dot(m_ref[...], w_ref[...])


def out_projection(merged, w_out, x, mod3, row_of_tile, k_gate, tm, tn):
    ntok, d = x.shape
    kd = merged.shape[1]
    return pl.pallas_call(
        _out_proj_kernel,
        grid=(ntok // tm, d // tn),
        in_specs=[pl.BlockSpec((tm, kd), lambda i, j: (i, 0)),
                  pl.BlockSpec((kd, tn), lambda i, j: (0, j)),
                  pl.BlockSpec((tm, tn), lambda i, j: (i, j)),
                  pl.BlockSpec((1, 1, tn), lambda i, j: (row_of_tile(i, tm) * 6 + k_gate, 0, j))],
        out_specs=pl.BlockSpec((tm, tn), lambda i, j: (i, j)),
        out_shape=jax.ShapeDtypeStruct((ntok, d), f32),
        compiler_params=_params("parallel", "arbitrary"),
        name="out_projection",
    )(merged, w_out, x, mod3)


def _extract_round(r, x_ref, rank_ref, vals_ref, pos, sentinel):
    x = x_ref[...]
    m = jnp.max(x, axis=0, keepdims=True)
    idx = jnp.min(jnp.where(x == m, pos, sentinel), axis=0, keepdims=True)
    sel = pos == idx
    rank_ref[...] = jnp.where(sel, r.astype(f32), rank_ref[...])
    x_ref[...] = jnp.where(sel, -jnp.inf, x)
    vals_ref[pl.ds(r, 1), :] = m


def _candidate_layout(k):
    slabs = []
    for a in range(k // 2):
        n_valid = k // (a + 1)
        slabs.append((a, n_valid, -(-n_valid // SUBLANES) * SUBLANES))
    return slabs


def _peer_route_kernel(q_ref, keys_ref, n1_ref, e1_ref, r2_ref, e2_ref, x1_scr, x2_scr, v1_scr, v2_scr,
                       c_scr, crank_scr, cv_scr, rank2_scr, e2_scr, *, half):
    k = PEER_TOPK
    n_keys, t = x1_scr.shape
    s1 = _dot_nt(keys_ref[0, 0].astype(bf16), q_ref[:, :half].astype(bf16))
    s2 = _dot_nt(keys_ref[0, 1].astype(bf16), q_ref[:, half:].astype(bf16))
    x1_scr[...] = s1
    x2_scr[...] = s2
    e1_ref[0] = jnp.exp(s1 - jnp.max(s1, axis=0, keepdims=True))
    e2_scr[...] = jnp.exp(s2 - jnp.max(s2, axis=0, keepdims=True))
    n1_ref[0] = jnp.full((n_keys, t), float(k), f32)
    rank2_scr[...] = jnp.full((n_keys, t), float(k), f32)
    key_pos = lax.broadcasted_iota(jnp.int32, (n_keys, t), 0).astype(f32)

    def sub_round(r, carry):
        _extract_round(r, x1_scr, n1_ref.at[0], v1_scr, key_pos, float(n_keys))
        _extract_round(r, x2_scr, rank2_scr, v2_scr, key_pos, float(n_keys))
        return carry

    lax.fori_loop(0, k, sub_round, 0)

    v1 = v1_scr[...]
    v2 = v2_scr[...]
    slabs = _candidate_layout(k)
    cands, poss = [], []
    for a, n_valid, n_rows in slabs:
        b = lax.broadcasted_iota(jnp.int32, (n_rows, t), 0)
        cands.append(jnp.where(b < n_valid, v1[a:a + 1, :] + v2[:n_rows, :], -jnp.inf))
        poss.append((a * k + b).astype(f32))
    a_hi = lax.broadcasted_iota(jnp.int32, (k // 2, t), 0) + k // 2
    cands.append(v1[k // 2:, :] + v2[0:1, :])
    poss.append((a_hi * k).astype(f32))
    c_scr[...] = jnp.concatenate(cands, axis=0)
    cand_pos = jnp.concatenate(poss, axis=0)
    crank_scr[...] = jnp.full(c_scr.shape, float(k), f32)

    def cand_round(r, carry):
        _extract_round(r, c_scr, crank_scr, cv_scr, cand_pos, float(k * k))
        return carry

    lax.fori_loop(0, k, cand_round, 0)

    picked = jnp.where(crank_scr[...] < float(k), 1.0, 0.0)
    cnt = []
    row0 = 0
    for a, n_valid, n_rows in slabs:
        cnt.append(jnp.sum(picked[row0:row0 + n_rows, :], axis=0, keepdims=True))
        row0 += n_rows
    for a in range(k // 2, k):
        cnt.append(picked[row0 + a - k // 2:row0 + a - k // 2 + 1, :])
    cvals = cv_scr[...]
    z = jnp.sum(jnp.exp(cvals - cvals[0:1, :]), axis=0, keepdims=True)
    rank1 = n1_ref[0]
    n1 = jnp.zeros_like(rank1)
    for a in range(k):
        n1 = jnp.where(rank1 == float(a), cnt[a], n1)
    n1_ref[0] = n1
    r2_ref[0] = rank2_scr[...].astype(bf16)
    e2_ref[0] = (e2_scr[...] / z).astype(bf16)


def peer_route(qp, sub_keys, t_tile):
    ntok = qp.shape[0]
    heads, _, n_keys, half = sub_keys.shape
    k = PEER_TOPK
    n_cand = sum(s[2] for s in _candidate_layout(k)) + k // 2
    out_spec = pl.BlockSpec((1, n_keys, t_tile), lambda i, h: (h, 0, i))
    out_shape = jax.ShapeDtypeStruct((heads, n_keys, ntok), f32)
    out_shape_b = jax.ShapeDtypeStruct((heads, n_keys, ntok), bf16)
    return pl.pallas_call(
        functools.partial(_peer_route_kernel, half=half),
        grid=(ntok // t_tile, heads),
        in_specs=[pl.BlockSpec((t_tile, 2 * half), lambda i, h: (i, h)),
                  pl.BlockSpec((1, 2, n_keys, half), lambda i, h: (h, 0, 0, 0))],
        out_specs=[out_spec] * 4,
        out_shape=[out_shape, out_shape, out_shape_b, out_shape_b],
        scratch_shapes=[pltpu.VMEM((n_keys, t_tile), f32), pltpu.VMEM((n_keys, t_tile), f32),
                        pltpu.VMEM((k, t_tile), f32), pltpu.VMEM((k, t_tile), f32),
                        pltpu.VMEM((n_cand, t_tile), f32), pltpu.VMEM((n_cand, t_tile), f32),
                        pltpu.VMEM((k, t_tile), f32),
                        pltpu.VMEM((n_keys, t_tile), f32), pltpu.VMEM((n_keys, t_tile), f32)],
        compiler_params=_params("parallel", "parallel"),
        name="peer_route",
    )(qp, sub_keys)


BF16_SUBLANES = 2 * SUBLANES


def _bcast_rows_bf16(row, n_rows):
    tile = jnp.broadcast_to(row, (BF16_SUBLANES, row.shape[1])).astype(bf16)
    return jnp.concatenate([tile] * (n_rows // BF16_SUBLANES), axis=0)


def _peer_dense_kernel(ht_ref, u_ref, v_ref, n1_ref, e1_ref, r2_ref, e2_ref, acc_ref, coef_scr, *, heads, n_keys):
    e = pl.program_id(1)
    et = u_ref.shape[0]
    t = ht_ref.shape[1]
    cpt = et // n_keys

    @pl.when(e == 0)
    def _():
        acc_ref[...] = jnp.zeros_like(acc_ref)

    act = _dot(u_ref[...], ht_ref[...])
    for c in range(cpt):
        i = e * cpt + c
        n1_rows = [n1_ref[h, pl.ds(i, 1), :] for h in range(heads)]
        e1_rows = [e1_ref[h, pl.ds(i, 1), :] for h in range(heads)]
        for tk in range(t // LANES):
            cols = slice(tk * LANES, (tk + 1) * LANES)
            gate = None
            for h in range(heads):
                n1 = _bcast_rows_bf16(n1_rows[h][:, cols], n_keys)
                e1 = _bcast_rows_bf16(e1_rows[h][:, cols], n_keys)
                term = jnp.where(r2_ref[h, :, cols] < n1, e2_ref[h, :, cols] * e1, jnp.zeros((), bf16))
                gate = term if gate is None else gate + term
            a = act[c * n_keys:(c + 1) * n_keys, cols]
            gelu = 0.5 * a * (1.0 + lax.erf(a * (2.0 ** -0.5)))
            coef_scr[c, cols, :] = (gate.astype(f32) * gelu).astype(bf16).T
    lhs = jnp.concatenate([coef_scr[c] for c in range(cpt)], axis=-1)
    acc_ref[...] += _dot(lhs, v_ref[...])


def peer_dense(h_t, u, v, n1, e1, r2, e2, t_tile, e_tile):
    d, ntok = h_t.shape
    n_exp = u.shape[0]
    heads, n_keys, _ = n1.shape
    once = pl.Buffered(1)
    route_spec = pl.BlockSpec((heads, n_keys, t_tile), lambda t, e: (0, 0, t), pipeline_mode=once)
    return pl.pallas_call(
        functools.partial(_peer_dense_kernel, heads=heads, n_keys=n_keys),
        grid=(ntok // t_tile, n_exp // e_tile),
        in_specs=[pl.BlockSpec((d, t_tile), lambda t, e: (0, t), pipeline_mode=once),
                  pl.BlockSpec((e_tile, d), lambda t, e: (e, 0)),
                  pl.BlockSpec((e_tile, d), lambda t, e: (e, 0)),
                  route_spec, route_spec, route_spec, route_spec],
        out_specs=pl.BlockSpec((t_tile, d), lambda t, e: (t, 0)),
        out_shape=jax.ShapeDtypeStruct((ntok, d), f32),
        scratch_shapes=[pltpu.VMEM((e_tile // n_keys, t_tile, n_keys), bf16)],
        compiler_params=_params("parallel", "arbitrary"),
        name="peer_dense",
    )(h_t, u, v, n1, e1, r2, e2)


def _residual_kernel(x_ref, p_ref, g_ref, o_ref):
    o_ref[...] = x_ref[...] + g_ref[0] * p_ref[...]


def _residual_norm_kernel(x_ref, p_ref, g_ref, w_ref, o_ref):
    x = x_ref[...] + g_ref[0] * p_ref[...]
    y = x * lax.rsqrt(jnp.mean(x * x, axis=-1, keepdims=True) + NORM_EPS)
    o_ref[...] = y * w_ref[...]


def gated_residual(x, p, mod3, row_of_tile, k_gate, tm, norm_w=None):
    ntok, d = x.shape
    row_spec = pl.BlockSpec((tm, d), lambda i: (i, 0))
    in_specs = [row_spec, row_spec, pl.BlockSpec((1, 1, d), lambda i: (row_of_tile(i, tm) * 6 + k_gate, 0, 0))]
    args = [x, p, mod3]
    kern = _residual_kernel
    if norm_w is not None:
        in_specs.append(pl.BlockSpec((1, d), lambda i: (0, 0)))
        args.append(norm_w.reshape(1, d))
        kern = _residual_norm_kernel
    return pl.pallas_call(
        kern,
        grid=(ntok // tm,),
        in_specs=in_specs,
        out_specs=row_spec,
        out_shape=jax.ShapeDtypeStruct((ntok, d), f32),
        compiler_params=_params("parallel"),
        name="gated_residual",
    )(*args)


def _axial_rope_tables(n_tokens, hd):
    rows = n_tokens // GRID_W
    row = jnp.repeat(jnp.arange(rows), GRID_W).astype(f32)
    col = jnp.tile(jnp.arange(GRID_W), rows).astype(f32)
    half = hd // 2
    inv = 1.0 / (ROPE_BASE ** (jnp.arange(0, half, 2, dtype=f32) / half))
    ar = row[:, None] * inv[None, :]
    ac = col[:, None] * inv[None, :]
    ang = jnp.concatenate([ar, ar, ac, ac], axis=-1)
    return jnp.cos(ang), jnp.sin(ang)


def kernel(x_prompt, x_sample, cache_attn_k, cache_attn_v, state_ret_fwd, state_ret_bwd, c, c_ctx, w_ada, b_ada,
           norm_mix, norm_ffn, w_in, attn_sink, ret_decay_logit, w_proj_attn, w_proj_ret, w_out, peer_w_query,
           peer_sub_keys, peer_u, peer_v, norm_final):
    batch, seq, d = x_prompt.shape
    dbatch, dseq, _ = x_sample.shape
    depth = w_in.shape[0]
    assert depth >= 1
    n_heads = attn_sink.shape[1]
    _, _, past, n_kv, hd = cache_attn_k.shape
    ret_heads, ret_dk, ret_dv = state_ret_fwd.shape[2:]
    assert ret_dk == ret_dv
    peer_heads, peer_dkey = peer_w_query.shape[2:]
    q_w = n_heads * hd
    kv_w = n_kv * hd
    r_w = ret_heads * ret_dk
    ret_col0 = q_w + 2 * kv_w
    gate_col0 = ret_col0 + 5 * r_w

    n_prompt = batch * seq
    n_lat = dbatch * dseq
    ntok = n_prompt + n_lat

    def row_of_tile(i, tm):
        return jnp.where(i < n_prompt // tm, 0, 1 + (i - n_prompt // tm) // (dseq // tm))

    x = jnp.concatenate([x_prompt.reshape(n_prompt, d), x_sample.reshape(n_lat, d)], axis=0)
    n_cond = -(-(dbatch + 1) // 8) * 8
    cond = jnp.zeros((n_cond, d), f32).at[0].set(c_ctx).at[1:dbatch + 1].set(c)
    cos, sin = _axial_rope_tables(dseq, hd)
    assert n_prompt % dseq == 0
    tm_small = _tile(512, n_prompt, dseq)
    tm_big = _tile(1024, n_prompt, dseq)
    tn_in = _tile(1024, w_in.shape[2])
    tn_gate = _tile(512, d, gate_col0)
    tn_q = _tile(1024, peer_heads * peer_dkey)

    new_k, new_v, new_sf, new_sb = [], [], [], []
    for l in range(depth):
        mod3 = ada_modulation(cond, w_ada[l], b_ada[l]).reshape(n_cond * 6, 1, d)
        sink = attn_sink[l].astype(f32)
        log_gamma = jax.nn.log_sigmoid(ret_decay_logit[l].astype(f32))

        proj = norm_mod_matmul(x, norm_mix[l], mod3, row_of_tile, 0, 1, w_in[l].astype(bf16), tm_small, tn_in,
                               False)

        o_att = jnp.zeros((ntok, q_w), bf16)
        o_att = context_attention(proj, sink, o_att, batch, seq, n_heads, n_kv, hd)
        o_att = latent_attention(proj, sink, cache_attn_k[:, l].reshape(dbatch, past, kv_w),
                                 cache_attn_v[:, l].reshape(dbatch, past, kv_w), cos, sin, o_att,
                                 n_prompt, dbatch, dseq, n_heads, n_kv, hd)

        o_ret = jnp.zeros((ntok, r_w), bf16)
        zero_state = jnp.zeros((batch, ret_heads, ret_dk, ret_dv), f32)
        o_ret, s_f, s_b = retention(proj, log_gamma, zero_state, zero_state, o_ret, 0, batch, seq,
                                    ret_col0, ret_heads, ret_dk)
        o_ret, _, _ = retention(proj, log_gamma, state_ret_fwd[:, l], state_ret_bwd[:, l], o_ret, n_prompt,
                                dbatch, dseq, ret_col0, ret_heads, ret_dk)

        merged = merge_branches(o_att, o_ret, w_proj_attn[l].astype(bf16), w_proj_ret[l].astype(bf16), proj,
                                gate_col0, tm_big, tn_gate)
        x = out_projection(merged, w_out[l].astype(bf16), x, mod3, row_of_tile, 2, tm_big, tn_gate)

        new_k.append(proj[:n_prompt, q_w:q_w + kv_w].reshape(batch, seq, n_kv, hd))
        new_v.append(proj[:n_prompt, q_w + kv_w:q_w + 2 * kv_w].reshape(batch, seq, n_kv, hd))
        new_sf.append(s_f)
        new_sb.append(s_b)

        wq = peer_w_query[l].reshape(d, peer_heads * peer_dkey).astype(bf16)
        qp, h2 = norm_mod_matmul(x, norm_ffn[l], mod3, row_of_tile, 3, 4, wq, tm_small, tn_q, True)
        n1, e1, r2, e2 = peer_route(qp, peer_sub_keys[l], _tile(512, ntok))
        p = peer_dense(h2.T, peer_u[l].astype(bf16), peer_v[l].astype(bf16), n1, e1, r2, e2,
                       _tile(512, ntok), _tile(512, peer_u.shape[1]))
        x = gated_residual(x, p, mod3, row_of_tile, 5, _tile(256, n_prompt, dseq),
                           norm_final if l == depth - 1 else None)

    y_prompt = x[:n_prompt].reshape(batch, seq, d)
    y_sample = x[n_prompt:].reshape(dbatch, dseq, d)
    return (y_prompt, y_sample, jnp.stack(new_k, axis=1), jnp.stack(new_v, axis=1),
            jnp.stack(new_sf, axis=1), jnp.stack(new_sb, axis=1))
```

```python
import functools

import jax
import jax.numpy as jnp
from jax import lax
from jax.experimental import pallas as pl
from jax.experimental.pallas import tpu as pltpu

GRID_W = 64
WINDOW = 128
ATT_BLOCK = 128
ROPE_BASE = 10000.0
RET_CHUNK = 128
PEER_TOPK = 16
NORM_EPS = 1e-6

LANES = 128
SUBLANES = 8
MXU_DIM = 256
VMEM_LIMIT_BYTES = 56 * 1024 * 1024
NEG_BIG = -1e30

f32 = jnp.float32
bf16 = jnp.bfloat16


def _params(*sem):
    return pltpu.CompilerParams(dimension_semantics=sem, vmem_limit_bytes=VMEM_LIMIT_BYTES)


def _tile(pref, *dims):
    t = pref
    while any(dim % t for dim in dims):
        t //= 2
    assert t >= LANES
    return t


def _dot(a, b):
    return jnp.dot(a, b, preferred_element_type=f32)


def _dot_nt(a, b):
    return lax.dot_general(a, b, (((1,), (1,)), ((), ())), preferred_element_type=f32)


def _dot_tn(a, b):
    return lax.dot_general(a, b, (((0,), (0,)), ((), ())), preferred_element_type=f32)


def _ada_kernel(cond_ref, w_ref, b_ref, o_ref):
    c = cond_ref[...]
    s = (c * jax.nn.sigmoid(c)).astype(bf16)
    o_ref[...] = _dot(s, w_ref[...].astype(bf16)) + b_ref[...]


def ada_modulation(cond, w_ada, b_ada):
    r, d = cond.shape
    n = w_ada.shape[1]
    tn = _tile(512, n)
    return pl.pallas_call(
        _ada_kernel,
        grid=(n // tn,),
        in_specs=[pl.BlockSpec((r, d), lambda j: (0, 0)),
                  pl.BlockSpec((d, tn), lambda j: (0, j)),
                  pl.BlockSpec((1, tn), lambda j: (0, j))],
        out_specs=pl.BlockSpec((r, tn), lambda j: (0, j)),
        out_shape=jax.ShapeDtypeStruct((r, n), f32),
        compiler_params=_params("parallel"),
        name="ada_modulation",
    )(cond, w_ada, b_ada.reshape(1, n))


def _norm_mod(x, g, sh, sc):
    y = x * lax.rsqrt(jnp.mean(x * x, axis=-1, keepdims=True) + NORM_EPS)
    y = y * g
    return y * (1.0 + sc) + sh


def _norm_mod_matmul_kernel(x_ref, g_ref, sh_ref, sc_ref, w_ref, o_ref, h_scr):
    @pl.when(pl.program_id(1) == 0)
    def _():
        h_scr[...] = _norm_mod(x_ref[...], g_ref[...], sh_ref[0], sc_ref[0]).astype(bf16)

    o_ref[...] = _dot(h_scr[...], w_ref[...])


def _norm_mod_matmul_h_kernel(x_ref, g_ref, sh_ref, sc_ref, w_ref, o_ref, h_ref):
    @pl.when(pl.program_id(1) == 0)
    def _():
        h_ref[...] = _norm_mod(x_ref[...], g_ref[...], sh_ref[0], sc_ref[0]).astype(bf16)

    o_ref[...] = _dot(h_ref[...], w_ref[...])


def norm_mod_matmul(x, g, mod3, row_of_tile, k_shift, k_scale, w, tm, tn, emit_h):
    ntok, d = x.shape
    n = w.shape[1]

    def mod_spec(k):
        return pl.BlockSpec((1, 1, d), lambda i, j: (row_of_tile(i, tm) * 6 + k, 0, 0))

    in_specs = [pl.BlockSpec((tm, d), lambda i, j: (i, 0)),
                pl.BlockSpec((1, d), lambda i, j: (0, 0)),
                mod_spec(k_shift), mod_spec(k_scale),
                pl.BlockSpec((d, tn), lambda i, j: (0, j))]
    o_spec = pl.BlockSpec((tm, tn), lambda i, j: (i, j))
    if emit_h:
        return pl.pallas_call(
            _norm_mod_matmul_h_kernel,
            grid=(ntok // tm, n // tn),
            in_specs=in_specs,
            out_specs=[o_spec, pl.BlockSpec((tm, d), lambda i, j: (i, 0))],
            out_shape=[jax.ShapeDtypeStruct((ntok, n), f32), jax.ShapeDtypeStruct((ntok, d), bf16)],
            compiler_params=_params("parallel", "arbitrary"),
            name="norm_mod_matmul_h",
        )(x, g.reshape(1, d), mod3, mod3, w)
    return pl.pallas_call(
        _norm_mod_matmul_kernel,
        grid=(ntok // tm, n // tn),
        in_specs=in_specs,
        out_specs=o_spec,
        out_shape=jax.ShapeDtypeStruct((ntok, n), f32),
        scratch_shapes=[pltpu.VMEM((tm, d), bf16)],
        compiler_params=_params("parallel", "arbitrary"),
        name="norm_mod_matmul",
    )(x, g.reshape(1, d), mod3, mod3, w)


def _softmax_pv(logits, values, sink):
    m = sink
    for lg in logits:
        m = jnp.maximum(m, jnp.max(lg, axis=-1, keepdims=True))
    den = jnp.exp(sink - m)
    acc = None
    for lg, v in zip(logits, values):
        p = jnp.exp(lg - m)
        den = den + jnp.sum(p, axis=-1, keepdims=True)
        pv = _dot(p.astype(bf16), v)
        acc = pv if acc is None else acc + pv
    return acc / den


def _ctx_attn_kernel(sink_ref, q_ref, k_ref, v_ref, o_ref, *, group, hd):
    kvh = pl.program_id(1)
    scale = hd ** -0.5
    k = k_ref[...].astype(bf16)
    v = v_ref[...].astype(bf16)
    for g in range(group):
        q = q_ref[:, g * hd:(g + 1) * hd].astype(bf16)
        lg = _dot_nt(q, k) * scale
        o = _softmax_pv([lg], [v], sink_ref[kvh * group + g])
        o_ref[:, g * hd:(g + 1) * hd] = o.astype(o_ref.dtype)


def context_attention(proj, sink, nb, seq, n_heads, n_kv, hd):
    group = n_heads // n_kv
    q_w = n_heads * hd
    kcol = q_w // hd
    vcol = (q_w + n_kv * hd) // hd
    return pl.pallas_call(
        functools.partial(_ctx_attn_kernel, group=group, hd=hd),
        grid=(nb, n_kv),
        in_specs=[pl.BlockSpec(memory_space=pltpu.SMEM),
                  pl.BlockSpec((seq, group * hd), lambda b, h: (b, h)),
                  pl.BlockSpec((seq, hd), lambda b, h: (b, kcol + h)),
                  pl.BlockSpec((seq, hd), lambda b, h: (b, vcol + h))],
        out_specs=pl.BlockSpec((seq, group * hd), lambda b, h: (b, h)),
        out_shape=jax.ShapeDtypeStruct((nb * seq, q_w), bf16),
        compiler_params=_params("parallel", "parallel"),
        name="context_attention",
    )(sink, proj, proj, proj)


def _rope(x, cos, sin, first_half):
    rot = jnp.where(first_half, -pltpu.roll(x, 3 * (LANES // 4), 1), pltpu.roll(x, LANES // 4, 1))
    return x * cos + rot * sin


def _lat_attn_kernel(sink_ref, q_ref, kp_ref, kc_ref, kn_ref, vp_ref, vc_ref, vn_ref, kx_ref, vx_ref,
                     cp_ref, cc_ref, cn_ref, sp_ref, sc_ref, sn_ref, o_ref, *, group, hd):
    n = pl.program_id(1)
    kvh = pl.program_id(2)
    nblk = pl.num_programs(1)
    scale = hd ** -0.5
    blk = (ATT_BLOCK, hd)
    lane = lax.broadcasted_iota(jnp.int32, blk, 1)
    first_half = (lane % (hd // 2)) < (hd // 4)
    row = lax.broadcasted_iota(jnp.int32, (ATT_BLOCK, ATT_BLOCK), 0)
    col = lax.broadcasted_iota(jnp.int32, (ATT_BLOCK, ATT_BLOCK), 1)
    mask_p = jnp.logical_and(col >= row, n > 0)
    mask_n = jnp.logical_and(col <= row, n < nblk - 1)
    kp = _rope(kp_ref[...], cp_ref[...], sp_ref[...], first_half).astype(bf16)
    kc = _rope(kc_ref[...], cc_ref[...], sc_ref[...], first_half).astype(bf16)
    kn = _rope(kn_ref[...], cn_ref[...], sn_ref[...], first_half).astype(bf16)
    kx = kx_ref[0].astype(bf16)
    values = [vp_ref[...].astype(bf16), vc_ref[...].astype(bf16), vn_ref[...].astype(bf16), vx_ref[0].astype(bf16)]
    for g in range(group):
        q = _rope(q_ref[:, g * hd:(g + 1) * hd], cc_ref[...], sc_ref[...], first_half).astype(bf16)
        lp = jnp.where(mask_p, _dot_nt(q, kp) * scale, NEG_BIG)
        lc = _dot_nt(q, kc) * scale
        ln = jnp.where(mask_n, _dot_nt(q, kn) * scale, NEG_BIG)
        lx = _dot_nt(q, kx) * scale
        o = _softmax_pv([lp, lc, ln, lx], values, sink_ref[kvh * group + g])
        o_ref[:, g * hd:(g + 1) * hd] = o.astype(o_ref.dtype)


def latent_attention(proj, sink, k_ctx, v_ctx, cos, sin, nb, seq, n_heads, n_kv, hd):
    assert WINDOW == ATT_BLOCK
    group = n_heads // n_kv
    q_w = n_heads * hd
    kcol = q_w // hd
    vcol = (q_w + n_kv * hd) // hd
    nblk = seq // ATT_BLOCK
    past = k_ctx.shape[1]

    def rb(b, n):
        return b * nblk + n

    def prev(n):
        return jnp.maximum(n - 1, 0)

    def nxt(n):
        return jnp.minimum(n + 1, nblk - 1)

    def kv_spec(col0, f):
        return pl.BlockSpec((ATT_BLOCK, hd), lambda b, n, h: (rb(b, f(n)), col0 + h))

    def tab_spec(f):
        return pl.BlockSpec((ATT_BLOCK, hd), lambda b, n, h: (f(n), 0))

    ident = lambda n: n
    ctx_spec = pl.BlockSpec((1, past, hd), lambda b, n, h: (b, 0, h))
    return pl.pallas_call(
        functools.partial(_lat_attn_kernel, group=group, hd=hd),
        grid=(nb, nblk, n_kv),
        in_specs=[pl.BlockSpec(memory_space=pltpu.SMEM),
                  pl.BlockSpec((ATT_BLOCK, group * hd), lambda b, n, h: (rb(b, n), h)),
                  kv_spec(kcol, prev), kv_spec(kcol, ident), kv_spec(kcol, nxt),
                  kv_spec(vcol, prev), kv_spec(vcol, ident), kv_spec(vcol, nxt),
                  ctx_spec, ctx_spec,
                  tab_spec(prev), tab_spec(ident), tab_spec(nxt),
                  tab_spec(prev), tab_spec(ident), tab_spec(nxt)],
        out_specs=pl.BlockSpec((ATT_BLOCK, group * hd), lambda b, n, h: (rb(b, n), h)),
        out_shape=jax.ShapeDtypeStruct((nb * seq, q_w), bf16),
        compiler_params=_params("parallel", "parallel", "parallel"),
        name="latent_attention",
    )(sink, proj, proj, proj, proj, proj, proj, proj, k_ctx, v_ctx, cos, cos, cos, sin, sin, sin)


RET_HEADS_PER_STEP = 2


def _ret_kernel(lg_ref, q_ref, k_ref, v_ref, gf_ref, gb_ref, s0f_ref, s0b_ref, o_ref, sf_ref, sb_ref,
                of_scr, ob_scr, *, n_chunks, dk, hps):
    hb = pl.program_id(1)
    c = RET_CHUNK
    ii = lax.broadcasted_iota(jnp.int32, (c, c), 0).astype(f32)
    jj = lax.broadcasted_iota(jnp.int32, (c, c), 1).astype(f32)
    d = ii - jj
    ri = lax.broadcasted_iota(jnp.int32, (c, dk), 0).astype(f32)
    kscale = dk ** -0.5

    consts = []
    for hh in range(hps):
        lgf = lg_ref[0, hb * hps + hh]
        lgb = lg_ref[1, hb * hps + hh]
        fwd = (jnp.where(d >= 0, jnp.exp(jnp.maximum(d, 0.0) * lgf), 0.0),
               jnp.exp((ri + 1.0) * lgf), jnp.exp((c - 1.0 - ri) * lgf), jnp.exp(jnp.full((dk, dk), c * lgf, f32)))
        bwd = (jnp.where(d <= 0, jnp.exp(jnp.maximum(-d, 0.0) * lgb), 0.0),
               jnp.exp((c - ri) * lgb), jnp.exp(ri * lgb), jnp.exp(jnp.full((dk, dk), c * lgb, f32)))
        consts.append((fwd, bwd))

    def chunk(n, hh, state, cst, o_scr):
        decay, qdec, kdec, cd = cst
        r = pl.ds(pl.multiple_of(n * c, c), c)
        cs = slice(hh * dk, (hh + 1) * dk)
        q = q_ref[r, cs]
        k = k_ref[r, cs] * kscale
        vb = v_ref[r, cs].astype(bf16)
        scores = _dot_nt(q.astype(bf16), k.astype(bf16)) * decay
        o = _dot(scores.astype(bf16), vb) + _dot((q * qdec).astype(bf16), state.astype(bf16))
        o_scr[r, cs] = o
        return cd * state + _dot((k * kdec).T.astype(bf16), vb)

    def body(n, states):
        new = []
        for hh in range(hps):
            sf, sb = states[hh]
            new.append((chunk(n, hh, sf, consts[hh][0], of_scr),
                        chunk(n_chunks - 1 - n, hh, sb, consts[hh][1], ob_scr)))
        return tuple(new)

    init = tuple((s0f_ref[0, hh], s0b_ref[0, hh]) for hh in range(hps))
    final = lax.fori_loop(0, n_chunks, body, init)
    for hh in range(hps):
        sf_ref[0, hh] = final[hh][0]
        sb_ref[0, hh] = final[hh][1]

    def hn(o):
        return o * lax.rsqrt(jnp.mean(o * o, axis=-1, keepdims=True) + NORM_EPS)

    def finish(n, carry):
        r = pl.ds(pl.multiple_of(n * c, c), c)
        for hh in range(hps):
            cs = slice(hh * dk, (hh + 1) * dk)
            gf = gf_ref[r, cs]
            gb = gb_ref[r, cs]
            out = hn(of_scr[r, cs]) * (gf * jax.nn.sigmoid(gf)) + hn(ob_scr[r, cs]) * (gb * jax.nn.sigmoid(gb))
            o_ref[r, cs] = out.astype(o_ref.dtype)
        return carry

    lax.fori_loop(0, n_chunks, finish, 0)


def retention(proj, log_gamma, s0f, s0b, nb, seq, col0, n_heads, dk):
    n_chunks = seq // RET_CHUNK
    hps = RET_HEADS_PER_STEP if n_heads % RET_HEADS_PER_STEP == 0 else 1
    w = hps * dk
    c0 = col0 // w
    gpb = n_heads // hps

    def col_spec(k):
        return pl.BlockSpec((seq, w), lambda b, h: (b, c0 + k * gpb + h))

    st_spec = pl.BlockSpec((1, hps, dk, dk), lambda b, h: (b, h, 0, 0))
    st_shape = jax.ShapeDtypeStruct((nb, n_heads, dk, dk), f32)
    return pl.pallas_call(
        functools.partial(_ret_kernel, n_chunks=n_chunks, dk=dk, hps=hps),
        grid=(nb, gpb),
        in_specs=[pl.BlockSpec(memory_space=pltpu.SMEM),
                  col_spec(0), col_spec(1), col_spec(2), col_spec(3), col_spec(4),
                  st_spec, st_spec],
        out_specs=[pl.BlockSpec((seq, w), lambda b, h: (b, h)), st_spec, st_spec],
        out_shape=[jax.ShapeDtypeStruct((nb * seq, n_heads * dk), bf16), st_shape, st_shape],
        scratch_shapes=[pltpu.VMEM((seq, w), f32), pltpu.VMEM((seq, w), f32)],
        compiler_params=_params("parallel", "parallel"),
        name="retention",
    )(log_gamma, proj, proj, proj, proj, proj, s0f, s0b)


def _merge_kernel(oa_ref, or_ref, wa_ref, wr_ref, ga_ref, gr_ref, o_ref):
    a = _dot(oa_ref[...], wa_ref[...])
    r = _dot(or_ref[...], wr_ref[...])
    o_ref[...] = (jax.nn.sigmoid(ga_ref[...]) * a + jax.nn.sigmoid(gr_ref[...]) * r).astype(o_ref.dtype)


def merge_branches(o_att, o_ret, w_pa, w_pr, proj, gate_col0, tm, tn):
    ntok, qw = o_att.shape
    rw = o_ret.shape[1]
    d = w_pa.shape[1]
    ga0 = gate_col0 // tn
    gr0 = (gate_col0 + d) // tn
    return pl.pallas_call(
        _merge_kernel,
        grid=(ntok // tm, d // tn),
        in_specs=[pl.BlockSpec((tm, qw), lambda i, j: (i, 0)),
                  pl.BlockSpec((tm, rw), lambda i, j: (i, 0)),
                  pl.BlockSpec((qw, tn), lambda i, j: (0, j)),
                  pl.BlockSpec((rw, tn), lambda i, j: (0, j)),
                  pl.BlockSpec((tm, tn), lambda i, j: (i, ga0 + j)),
                  pl.BlockSpec((tm, tn), lambda i, j: (i, gr0 + j))],
        out_specs=pl.BlockSpec((tm, tn), lambda i, j: (i, j)),
        out_shape=jax.ShapeDtypeStruct((ntok, d), bf16),
        compiler_params=_params("parallel", "arbitrary"),
        name="merge_branches",
    )(o_att, o_ret, w_pa, w_pr, proj, proj)


def _out_proj_kernel(m_ref, w_ref, x_ref, g_ref, o_ref):
    o_ref[...] = x_ref[...] + g_ref[0] * _dot(m_ref[...], w_ref[...])


def out_projection(merged, w_out, x, mod3, row_of_tile, k_gate, tm, tn):
    ntok, d = x.shape
    kd = merged.shape[1]
    return pl.pallas_call(
        _out_proj_kernel,
        grid=(ntok // tm, d // tn),
        in_specs=[pl.BlockSpec((tm, kd), lambda i, j: (i, 0)),
                  pl.BlockSpec((kd, tn), lambda i, j: (0, j)),
                  pl.BlockSpec((tm, tn), lambda i, j: (i, j)),
                  pl.BlockSpec((1, 1, tn), lambda i, j: (row_of_tile(i, tm) * 6 + k_gate, 0, j))],
        out_specs=pl.BlockSpec((tm, tn), lambda i, j: (i, j)),
        out_shape=jax.ShapeDtypeStruct((ntok, d), f32),
        compiler_params=_params("parallel", "arbitrary"),
        name="out_projection",
    )(merged, w_out, x, mod3)


def _extract_round(r, x_ref, rank_ref, vals_ref, pos, sentinel):
    x = x_ref[...]
    m = jnp.max(x, axis=0, keepdims=True)
    idx = jnp.min(jnp.where(x == m, pos, sentinel), axis=0, keepdims=True)
    sel = pos == idx
    rank_ref[...] = jnp.where(sel, lax.convert_element_type(r, f32), rank_ref[...])
    x_ref[...] = jnp.where(sel, -jnp.inf, x)
    vals_ref[pl.ds(r, 1), :] = m


def _candidate_layout(k):
    slabs = []
    for a in range(k // 2):
        n_valid = k // (a + 1)
        slabs.append((a, n_valid, -(-n_valid // SUBLANES) * SUBLANES))
    return slabs


def _peer_route_kernel(q_ref, keys_ref, n1_ref, e1_ref, r2_ref, e2_ref, x1_scr, x2_scr, v1_scr, v2_scr,
                       c_scr, crank_scr, cv_scr, rank2_scr, e2_scr, *, half):
    k = PEER_TOPK
    n_keys, t = x1_scr.shape
    s1 = _dot_nt(keys_ref[0, 0].astype(bf16), q_ref[:, :half].astype(bf16))
    s2 = _dot_nt(keys_ref[0, 1].astype(bf16), q_ref[:, half:].astype(bf16))
    x1_scr[...] = s1
    x2_scr[...] = s2
    e1_ref[0] = jnp.exp(s1 - jnp.max(s1, axis=0, keepdims=True))
    e2_scr[...] = jnp.exp(s2 - jnp.max(s2, axis=0, keepdims=True))
    n1_ref[0] = jnp.full((n_keys, t), float(k), f32)
    rank2_scr[...] = jnp.full((n_keys, t), float(k), f32)
    key_pos = lax.broadcasted_iota(jnp.int32, (n_keys, t), 0).astype(f32)

    def sub_round(r, carry):
        _extract_round(r, x1_scr, n1_ref.at[0], v1_scr, key_pos, float(n_keys))
        _extract_round(r, x2_scr, rank2_scr, v2_scr, key_pos, float(n_keys))
        return carry

    lax.fori_loop(0, k, sub_round, 0)

    v1 = v1_scr[...]
    v2 = v2_scr[...]
    slabs = _candidate_layout(k)
    cands, poss = [], []
    for a, n_valid, n_rows in slabs:
        b = lax.broadcasted_iota(jnp.int32, (n_rows, t), 0)
        cands.append(jnp.where(b < n_valid, v1[a:a + 1, :] + v2[:n_rows, :], -jnp.inf))
        poss.append((a * k + b).astype(f32))
    a_hi = lax.broadcasted_iota(jnp.int32, (k // 2, t), 0) + k // 2
    cands.append(v1[k // 2:, :] + v2[0:1, :])
    poss.append((a_hi * k).astype(f32))
    c_scr[...] = jnp.concatenate(cands, axis=0)
    cand_pos = jnp.concatenate(poss, axis=0)
    crank_scr[...] = jnp.full(c_scr.shape, float(k), f32)

    def cand_round(r, carry):
        _extract_round(r, c_scr, crank_scr, cv_scr, cand_pos, float(k * k))
        return carry

    lax.fori_loop(0, k, cand_round, 0)

    picked = jnp.where(crank_scr[...] < float(k), 1.0, 0.0)
    cnt = []
    row0 = 0
    for a, n_valid, n_rows in slabs:
        cnt.append(jnp.sum(picked[row0:row0 + n_rows, :], axis=0, keepdims=True))
        row0 += n_rows
    for a in range(k // 2, k):
        cnt.append(picked[row0 + a - k // 2:row0 + a - k // 2 + 1, :])
    cvals = cv_scr[...]
    z = jnp.sum(jnp.exp(cvals - cvals[0:1, :]), axis=0, keepdims=True)
    rank1 = n1_ref[0]
    n1 = jnp.zeros_like(rank1)
    for a in range(k):
        n1 = jnp.where(rank1 == float(a), cnt[a], n1)
    n1_ref[0] = n1
    r2_ref[0] = rank2_scr[...].astype(bf16)
    e2_ref[0] = (e2_scr[...] / z).astype(bf16)


def peer_route(qp, sub_keys, t_tile):
    ntok = qp.shape[0]
    heads, _, n_keys, half = sub_keys.shape
    k = PEER_TOPK
    n_cand = sum(s[2] for s in _candidate_layout(k)) + k // 2
    out_spec = pl.BlockSpec((1, n_keys, t_tile), lambda i, h: (h, 0, i))
    out_shape = jax.ShapeDtypeStruct((heads, n_keys, ntok), f32)
    out_shape_b = jax.ShapeDtypeStruct((heads, n_keys, ntok), bf16)
    return pl.pallas_call(
        functools.partial(_peer_route_kernel, half=half),
        grid=(ntok // t_tile, heads),
        in_specs=[pl.BlockSpec((t_tile, 2 * half), lambda i, h: (i, h)),
                  pl.BlockSpec((1, 2, n_keys, half), lambda i, h: (h, 0, 0, 0))],
        out_specs=[out_spec] * 4,
        out_shape=[out_shape, out_shape, out_shape_b, out_shape_b],
        scratch_shapes=[pltpu.VMEM((n_keys, t_tile), f32), pltpu.VMEM((n_keys, t_tile), f32),
                        pltpu.VMEM((k, t_tile), f32), pltpu.VMEM((k, t_tile), f32),
                        pltpu.VMEM((n_cand, t_tile), f32), pltpu.VMEM((n_cand, t_tile), f32),
                        pltpu.VMEM((k, t_tile), f32),
                        pltpu.VMEM((n_keys, t_tile), f32), pltpu.VMEM((n_keys, t_tile), f32)],
        compiler_params=_params("parallel", "parallel"),
        name="peer_route",
    )(qp, sub_keys)


BF16_SUBLANES = 2 * SUBLANES


def _bcast_rows_bf16(row, n_rows):
    tile = jnp.broadcast_to(row, (BF16_SUBLANES, row.shape[1])).astype(bf16)
    return jnp.concatenate([tile] * (n_rows // BF16_SUBLANES), axis=0)


def _peer_dense_kernel(ht_ref, u_ref, v_ref, n1_ref, e1_ref, r2_ref, e2_ref, acc_ref, coef_scr, r2_scr, e2_scr,
                       *, heads, n_keys):
    e = pl.program_id(1)
    et = u_ref.shape[0]
    t = ht_ref.shape[1]
    cpt = et // n_keys

    @pl.when(e == 0)
    def _():
        acc_ref[...] = jnp.zeros_like(acc_ref)
        r2_scr[...] = r2_ref[...]
        e2_scr[...] = e2_ref[...]

    act = _dot(u_ref[...], ht_ref[...])
    for c in range(cpt):
        i = e * cpt + c
        n1_rows = [n1_ref[h, pl.ds(i, 1), :] for h in range(heads)]
        e1_rows = [e1_ref[h, pl.ds(i, 1), :] for h in range(heads)]
        for tk in range(t // LANES):
            cols = slice(tk * LANES, (tk + 1) * LANES)
            gate = None
            for h in range(heads):
                n1 = _bcast_rows_bf16(n1_rows[h][:, cols], n_keys)
                e1 = _bcast_rows_bf16(e1_rows[h][:, cols], n_keys)
                term = jnp.where(r2_scr[h, :, cols] < n1, e2_scr[h, :, cols] * e1, jnp.zeros((), bf16))
                gate = term if gate is None else gate + term
            a = act[c * n_keys:(c + 1) * n_keys, cols].astype(bf16)
            gelu = 0.5 * a * (1.0 + lax.erf(a * (2.0 ** -0.5)))
            coef_scr[c, cols, :] = (gate * gelu).T
    lhs = jnp.concatenate([coef_scr[c] for c in range(cpt)], axis=-1)
    acc_ref[...] += _dot(lhs, v_ref[...])


def peer_dense(h_t, u, v, n1, e1, r2, e2, t_tile, e_tile):
    d, ntok = h_t.shape
    n_exp = u.shape[0]
    heads, n_keys, _ = n1.shape
    once = pl.Buffered(1)
    route_spec = pl.BlockSpec((heads, n_keys, t_tile), lambda t, e: (0, 0, t), pipeline_mode=once)
    return pl.pallas_call(
        functools.partial(_peer_dense_kernel, heads=heads, n_keys=n_keys),
        grid=(ntok // t_tile, n_exp // e_tile),
        in_specs=[pl.BlockSpec((d, t_tile), lambda t, e: (0, t), pipeline_mode=once),
                  pl.BlockSpec((e_tile, d), lambda t, e: (e, 0)),
                  pl.BlockSpec((e_tile, d), lambda t, e: (e, 0)),
                  route_spec, route_spec, route_spec, route_spec],
        out_specs=pl.BlockSpec((t_tile, d), lambda t, e: (t, 0)),
        out_shape=jax.ShapeDtypeStruct((ntok, d), f32),
        scratch_shapes=[pltpu.VMEM((e_tile // n_keys, t_tile, n_keys), bf16),
                        pltpu.VMEM((heads, n_keys, t_tile), bf16),
                        pltpu.VMEM((heads, n_keys, t_tile), bf16)],
        compiler_params=_params("parallel", "arbitrary"),
        name="peer_dense",
    )(h_t, u, v, n1, e1, r2, e2)


def _residual_kernel(x_ref, p_ref, g_ref, o_ref):
    o_ref[...] = x_ref[...] + g_ref[0] * p_ref[...]


def _residual_norm_kernel(x_ref, p_ref, g_ref, w_ref, o_ref):
    x = x_ref[...] + g_ref[0] * p_ref[...]
    y = x * lax.rsqrt(jnp.mean(x * x, axis=-1, keepdims=True) + NORM_EPS)
    o_ref[...] = y * w_ref[...]


def gated_residual(x, p, mod3, row_of_tile, k_gate, tm, norm_w=None):
    ntok, d = x.shape
    row_spec = pl.BlockSpec((tm, d), lambda i: (i, 0))
    in_specs = [row_spec, row_spec, pl.BlockSpec((1, 1, d), lambda i: (row_of_tile(i, tm) * 6 + k_gate, 0, 0))]
    args = [x, p, mod3]
    kern = _residual_kernel
    if norm_w is not None:
        in_specs.append(pl.BlockSpec((1, d), lambda i: (0, 0)))
        args.append(norm_w.reshape(1, d))
        kern = _residual_norm_kernel
    return pl.pallas_call(
        kern,
        grid=(ntok // tm,),
        in_specs=in_specs,
        out_specs=row_spec,
        out_shape=jax.ShapeDtypeStruct((ntok, d), f32),
        compiler_params=_params("parallel"),
        name="gated_residual",
    )(*args)


def _axial_rope_tables(n_tokens, hd):
    rows = n_tokens // GRID_W
    row = jnp.repeat(jnp.arange(rows), GRID_W).astype(f32)
    col = jnp.tile(jnp.arange(GRID_W), rows).astype(f32)
    half = hd // 2
    inv = 1.0 / (ROPE_BASE ** (jnp.arange(0, half, 2, dtype=f32) / half))
    ar = row[:, None] * inv[None, :]
    ac = col[:, None] * inv[None, :]
    ang = jnp.concatenate([ar, ar, ac, ac], axis=-1)
    return jnp.cos(ang), jnp.sin(ang)


def kernel(x_prompt, x_sample, cache_attn_k, cache_attn_v, state_ret_fwd, state_ret_bwd, c, c_ctx, w_ada, b_ada,
           norm_mix, norm_ffn, w_in, attn_sink, ret_decay_logit, w_proj_attn, w_proj_ret, w_out, peer_w_query,
           peer_sub_keys, peer_u, peer_v, norm_final):
    batch, seq, d = x_prompt.shape
    dbatch, dseq, _ = x_sample.shape
    depth = w_in.shape[0]
    assert depth >= 1
    n_heads = attn_sink.shape[1]
    _, _, past, n_kv, hd = cache_attn_k.shape
    ret_heads, ret_dk, ret_dv = state_ret_fwd.shape[2:]
    assert ret_dk == ret_dv
    peer_heads, peer_dkey = peer_w_query.shape[2:]
    q_w = n_heads * hd
    kv_w = n_kv * hd
    r_w = ret_heads * ret_dk
    ret_col0 = q_w + 2 * kv_w
    gate_col0 = ret_col0 + 5 * r_w

    n_prompt = batch * seq
    n_lat = dbatch * dseq

    def cond_row_ctx(i, tm):
        return 0

    def cond_row_lat(i, tm):
        return 1 + i // (dseq // tm)

    n_cond = -(-(dbatch + 1) // 8) * 8
    cond = jnp.zeros((n_cond, d), f32).at[0].set(c_ctx).at[1:dbatch + 1].set(c)
    cos, sin = _axial_rope_tables(dseq, hd)
    tm_small = _tile(512, n_prompt, dseq)
    tm_big = _tile(1024, n_prompt, dseq)
    tm_res = _tile(256, n_prompt, dseq)
    tn_in = _tile(1024, w_in.shape[2])
    tn_gate = _tile(512, d, gate_col0)
    tn_q = _tile(1024, peer_heads * peer_dkey)
    e_tile = _tile(512, peer_u.shape[1])

    xp = x_prompt.reshape(n_prompt, d)
    xs = x_sample.reshape(n_lat, d)
    new_k, new_v, new_sf, new_sb = [], [], [], []
    for l in range(depth):
        last = l == depth - 1
        mod3 = ada_modulation(cond, w_ada[l], b_ada[l]).reshape(n_cond * 6, 1, d)
        sink = attn_sink[l].astype(f32)
        log_gamma = jax.nn.log_sigmoid(ret_decay_logit[l].astype(f32))
        w_in_b = w_in[l].astype(bf16)
        w_pa_b = w_proj_attn[l].astype(bf16)
        w_pr_b = w_proj_ret[l].astype(bf16)
        w_out_b = w_out[l].astype(bf16)
        wq_b = peer_w_query[l].reshape(d, peer_heads * peer_dkey).astype(bf16)
        u_b = peer_u[l].astype(bf16)
        v_b = peer_v[l].astype(bf16)

        def channel_mix(x, proj, o_att, o_ret, cond_row):
            merged = merge_branches(o_att, o_ret, w_pa_b, w_pr_b, proj, gate_col0, tm_big, tn_gate)
            x = out_projection(merged, w_out_b, x, mod3, cond_row, 2, tm_big, tn_gate)
            qp, h2 = norm_mod_matmul(x, norm_ffn[l], mod3, cond_row, 3, 4, wq_b, tm_small, tn_q, True)
            n1, e1, r2, e2 = peer_route(qp, peer_sub_keys[l], _tile(512, x.shape[0]))
            p = peer_dense(h2.T, u_b, v_b, n1, e1, r2, e2, _tile(512, x.shape[0]), e_tile)
            return gated_residual(x, p, mod3, cond_row, 5, tm_res, norm_final if last else None)

        proj = norm_mod_matmul(xp, norm_mix[l], mod3, cond_row_ctx, 0, 1, w_in_b, tm_small, tn_in, False)
        o_att = context_attention(proj, sink, batch, seq, n_heads, n_kv, hd)
        zero_state = jnp.zeros((batch, ret_heads, ret_dk, ret_dv), f32)
        o_ret, s_f, s_b = retention(proj, log_gamma, zero_state, zero_state, batch, seq, ret_col0, ret_heads, ret_dk)
        new_k.append(proj[:, q_w:q_w + kv_w].reshape(batch, seq, n_kv, hd))
        new_v.append(proj[:, q_w + kv_w:q_w + 2 * kv_w].reshape(batch, seq, n_kv, hd))
        new_sf.append(s_f)
        new_sb.append(s_b)
        xp = channel_mix(xp, proj, o_att, o_ret, cond_row_ctx)

        proj = norm_mod_matmul(xs, norm_mix[l], mod3, cond_row_lat, 0, 1, w_in_b, tm_small, tn_in, False)
        o_att = latent_attention(proj, sink, cache_attn_k[:, l].reshape(dbatch, past, kv_w),
                                 cache_attn_v[:, l].reshape(dbatch, past, kv_w), cos, sin,
                                 dbatch, dseq, n_heads, n_kv, hd)
        o_ret, _, _ = retention(proj, log_gamma, state_ret_fwd[:, l], state_ret_bwd[:, l], dbatch, dseq,
                                ret_col0, ret_heads, ret_dk)
        xs = channel_mix(xs, proj, o_att, o_ret, cond_row_lat)

    return (xp.reshape(batch, seq, d), xs.reshape(dbatch, dseq, d), jnp.stack(new_k, axis=1),
            jnp.stack(new_v, axis=1), jnp.stack(new_sf, axis=1), jnp.stack(new_sb, axis=1))
```

```python
import functools

import jax
import jax.numpy as jnp
from jax import lax
from jax.experimental import pallas as pl
from jax.experimental.pallas import tpu as pltpu

GRID_W = 64
WINDOW = 128
ATT_BLOCK = 128
ROPE_BASE = 10000.0
RET_CHUNK = 128
PEER_TOPK = 16
NORM_EPS = 1e-6

LANES = 128
SUBLANES = 8
MXU_DIM = 256
VMEM_LIMIT_BYTES = 56 * 1024 * 1024
NEG_BIG = -1e30

f32 = jnp.float32
bf16 = jnp.bfloat16


def _params(*sem):
    return pltpu.CompilerParams(dimension_semantics=sem, vmem_limit_bytes=VMEM_LIMIT_BYTES)


def _tile(pref, *dims):
    t = pref
    while any(dim % t for dim in dims):
        t //= 2
    assert t >= LANES
    return t


def _dot(a, b):
    return jnp.dot(a, b, preferred_element_type=f32)


def _dot_nt(a, b):
    return lax.dot_general(a, b, (((1,), (1,)), ((), ())), preferred_element_type=f32)


def _dot_tn(a, b):
    return lax.dot_general(a, b, (((0,), (0,)), ((), ())), preferred_element_type=f32)


def _ada_kernel(cond_ref, w_ref, b_ref, o_ref):
    c = cond_ref[...]
    s = (c * jax.nn.sigmoid(c)).astype(bf16)
    o_ref[...] = _dot(s, w_ref[...].astype(bf16)) + b_ref[...]


def ada_modulation(cond, w_ada, b_ada):
    r, d = cond.shape
    n = w_ada.shape[1]
    tn = _tile(512, n)
    return pl.pallas_call(
        _ada_kernel,
        grid=(n // tn,),
        in_specs=[pl.BlockSpec((r, d), lambda j: (0, 0)),
                  pl.BlockSpec((d, tn), lambda j: (0, j)),
                  pl.BlockSpec((1, tn), lambda j: (0, j))],
        out_specs=pl.BlockSpec((r, tn), lambda j: (0, j)),
        out_shape=jax.ShapeDtypeStruct((r, n), f32),
        compiler_params=_params("parallel"),
        name="ada_modulation",
    )(cond, w_ada, b_ada.reshape(1, n))


def _norm_mod(x, g, sh, sc):
    y = x * lax.rsqrt(jnp.mean(x * x, axis=-1, keepdims=True) + NORM_EPS)
    y = y * g
    return y * (1.0 + sc) + sh


def _norm_mod_matmul_kernel(x_ref, g_ref, sh_ref, sc_ref, w_ref, o_ref, h_scr):
    @pl.when(pl.program_id(1) == 0)
    def _():
        h_scr[...] = _norm_mod(x_ref[...], g_ref[...], sh_ref[0], sc_ref[0]).astype(bf16)

    o_ref[...] = _dot(h_scr[...], w_ref[...])


def _norm_mod_matmul_h_kernel(x_ref, g_ref, sh_ref, sc_ref, w_ref, o_ref, h_ref):
    @pl.when(pl.program_id(1) == 0)
    def _():
        h_ref[...] = _norm_mod(x_ref[...], g_ref[...], sh_ref[0], sc_ref[0]).astype(bf16)

    o_ref[...] = _dot(h_ref[...], w_ref[...])


def norm_mod_matmul(x, g, mod3, row_of_tile, k_shift, k_scale, w, tm, tn, emit_h):
    ntok, d = x.shape
    n = w.shape[1]

    def mod_spec(k):
        return pl.BlockSpec((1, 1, d), lambda i, j: (row_of_tile(i, tm) * 6 + k, 0, 0))

    in_specs = [pl.BlockSpec((tm, d), lambda i, j: (i, 0)),
                pl.BlockSpec((1, d), lambda i, j: (0, 0)),
                mod_spec(k_shift), mod_spec(k_scale),
                pl.BlockSpec((d, tn), lambda i, j: (0, j))]
    o_spec = pl.BlockSpec((tm, tn), lambda i, j: (i, j))
    if emit_h:
        return pl.pallas_call(
            _norm_mod_matmul_h_kernel,
            grid=(ntok // tm, n // tn),
            in_specs=in_specs,
            out_specs=[o_spec, pl.BlockSpec((tm, d), lambda i, j: (i, 0))],
            out_shape=[jax.ShapeDtypeStruct((ntok, n), f32), jax.ShapeDtypeStruct((ntok, d), bf16)],
            compiler_params=_params("parallel", "arbitrary"),
            name="norm_mod_matmul_h",
        )(x, g.reshape(1, d), mod3, mod3, w)
    return pl.pallas_call(
        _norm_mod_matmul_kernel,
        grid=(ntok // tm, n // tn),
        in_specs=in_specs,
        out_specs=o_spec,
        out_shape=jax.ShapeDtypeStruct((ntok, n), f32),
        scratch_shapes=[pltpu.VMEM((tm, d), bf16)],
        compiler_params=_params("parallel", "arbitrary"),
        name="norm_mod_matmul",
    )(x, g.reshape(1, d), mod3, mod3, w)


def _softmax_pv(logits, values, sink):
    m = sink
    for lg in logits:
        m = jnp.maximum(m, jnp.max(lg, axis=-1, keepdims=True))
    den = jnp.exp(sink - m)
    acc = None
    for lg, v in zip(logits, values):
        p = jnp.exp(lg - m)
        den = den + jnp.sum(p, axis=-1, keepdims=True)
        pv = _dot(p.astype(bf16), v)
        acc = pv if acc is None else acc + pv
    return acc / den


def _sink_column(sink_ref, kvh, group, rows):
    return jnp.concatenate([jnp.full((rows, 1), sink_ref[kvh * group + g], f32) for g in range(group)], axis=0)


def _ctx_attn_kernel(sink_ref, q_ref, k_ref, v_ref, o_ref, *, group, hd):
    kvh = pl.program_id(1)
    scale = hd ** -0.5
    k = k_ref[...].astype(bf16)
    v = v_ref[...].astype(bf16)
    for g in range(group):
        q = q_ref[:, g * hd:(g + 1) * hd].astype(bf16)
        lg = _dot_nt(q, k) * scale
        o = _softmax_pv([lg], [v], sink_ref[kvh * group + g])
        o_ref[:, g * hd:(g + 1) * hd] = o.astype(o_ref.dtype)


def context_attention(proj, sink, nb, seq, n_heads, n_kv, hd):
    group = n_heads // n_kv
    q_w = n_heads * hd
    kcol = q_w // hd
    vcol = (q_w + n_kv * hd) // hd
    return pl.pallas_call(
        functools.partial(_ctx_attn_kernel, group=group, hd=hd),
        grid=(nb, n_kv),
        in_specs=[pl.BlockSpec(memory_space=pltpu.SMEM),
                  pl.BlockSpec((seq, group * hd), lambda b, h: (b, h)),
                  pl.BlockSpec((seq, hd), lambda b, h: (b, kcol + h)),
                  pl.BlockSpec((seq, hd), lambda b, h: (b, vcol + h))],
        out_specs=pl.BlockSpec((seq, group * hd), lambda b, h: (b, h)),
        out_shape=jax.ShapeDtypeStruct((nb * seq, q_w), bf16),
        compiler_params=_params("parallel", "parallel"),
        name="context_attention",
    )(sink, proj, proj, proj)


def _rope(x, cos, sin, first_half):
    rot = jnp.where(first_half, -pltpu.roll(x, 3 * (LANES // 4), 1), pltpu.roll(x, LANES // 4, 1))
    return x * cos + rot * sin


def _lat_attn_kernel(sink_ref, q_ref, kp_ref, kc_ref, kn_ref, vp_ref, vc_ref, vn_ref, kx_ref, vx_ref,
                     cp_ref, cc_ref, cn_ref, sp_ref, sc_ref, sn_ref, o_ref, *, group, hd):
    n = pl.program_id(1)
    kvh = pl.program_id(2)
    nblk = pl.num_programs(1)
    scale = hd ** -0.5
    blk = (ATT_BLOCK, hd)
    lane = lax.broadcasted_iota(jnp.int32, blk, 1)
    first_half = (lane % (hd // 2)) < (hd // 4)
    row = lax.broadcasted_iota(jnp.int32, (group * ATT_BLOCK, ATT_BLOCK), 0) % ATT_BLOCK
    col = lax.broadcasted_iota(jnp.int32, (group * ATT_BLOCK, ATT_BLOCK), 1)
    mask_p = jnp.logical_and(col >= row, n > 0)
    mask_n = jnp.logical_and(col <= row, n < nblk - 1)
    kp = _rope(kp_ref[...], cp_ref[...], sp_ref[...], first_half).astype(bf16)
    kc = _rope(kc_ref[...], cc_ref[...], sc_ref[...], first_half).astype(bf16)
    kn = _rope(kn_ref[...], cn_ref[...], sn_ref[...], first_half).astype(bf16)
    kx = kx_ref[0].astype(bf16)
    values = [vp_ref[...].astype(bf16), vc_ref[...].astype(bf16), vn_ref[...].astype(bf16), vx_ref[0].astype(bf16)]
    q = jnp.concatenate([_rope(q_ref[:, g * hd:(g + 1) * hd], cc_ref[...], sc_ref[...], first_half).astype(bf16)
                         for g in range(group)], axis=0)
    lp = jnp.where(mask_p, _dot_nt(q, kp) * scale, NEG_BIG)
    lc = _dot_nt(q, kc) * scale
    ln = jnp.where(mask_n, _dot_nt(q, kn) * scale, NEG_BIG)
    lx = _dot_nt(q, kx) * scale
    o = _softmax_pv([lp, lc, ln, lx], values, _sink_column(sink_ref, kvh, group, ATT_BLOCK))
    for g in range(group):
        o_ref[:, g * hd:(g + 1) * hd] = o[g * ATT_BLOCK:(g + 1) * ATT_BLOCK].astype(o_ref.dtype)


def latent_attention(proj, sink, k_ctx, v_ctx, cos, sin, nb, seq, n_heads, n_kv, hd):
    assert WINDOW == ATT_BLOCK
    group = n_heads // n_kv
    q_w = n_heads * hd
    kcol = q_w // hd
    vcol = (q_w + n_kv * hd) // hd
    nblk = seq // ATT_BLOCK
    past = k_ctx.shape[1]

    def rb(b, n):
        return b * nblk + n

    def prev(n):
        return jnp.maximum(n - 1, 0)

    def nxt(n):
        return jnp.minimum(n + 1, nblk - 1)

    def kv_spec(col0, f):
        return pl.BlockSpec((ATT_BLOCK, hd), lambda b, n, h: (rb(b, f(n)), col0 + h))

    def tab_spec(f):
        return pl.BlockSpec((ATT_BLOCK, hd), lambda b, n, h: (f(n), 0))

    ident = lambda n: n
    ctx_spec = pl.BlockSpec((1, past, hd), lambda b, n, h: (b, 0, h))
    return pl.pallas_call(
        functools.partial(_lat_attn_kernel, group=group, hd=hd),
        grid=(nb, nblk, n_kv),
        in_specs=[pl.BlockSpec(memory_space=pltpu.SMEM),
                  pl.BlockSpec((ATT_BLOCK, group * hd), lambda b, n, h: (rb(b, n), h)),
                  kv_spec(kcol, prev), kv_spec(kcol, ident), kv_spec(kcol, nxt),
                  kv_spec(vcol, prev), kv_spec(vcol, ident), kv_spec(vcol, nxt),
                  ctx_spec, ctx_spec,
                  tab_spec(prev), tab_spec(ident), tab_spec(nxt),
                  tab_spec(prev), tab_spec(ident), tab_spec(nxt)],
        out_specs=pl.BlockSpec((ATT_BLOCK, group * hd), lambda b, n, h: (rb(b, n), h)),
        out_shape=jax.ShapeDtypeStruct((nb * seq, q_w), bf16),
        compiler_params=_params("parallel", "parallel", "parallel"),
        name="latent_attention",
    )(sink, proj, proj, proj, proj, proj, proj, proj, k_ctx, v_ctx, cos, cos, cos, sin, sin, sin)


RET_HEADS_PER_STEP = 2


def _ret_kernel(lg_ref, q_ref, k_ref, v_ref, gf_ref, gb_ref, s0f_ref, s0b_ref, o_ref, sf_ref, sb_ref,
                of_scr, ob_scr, *, n_chunks, dk, hps):
    hb = pl.program_id(1)
    c = RET_CHUNK
    ii = lax.broadcasted_iota(jnp.int32, (c, c), 0).astype(f32)
    jj = lax.broadcasted_iota(jnp.int32, (c, c), 1).astype(f32)
    d = ii - jj
    ri = lax.broadcasted_iota(jnp.int32, (c, dk), 0).astype(f32)
    kscale = dk ** -0.5

    consts = []
    for hh in range(hps):
        lgf = lg_ref[0, hb * hps + hh]
        lgb = lg_ref[1, hb * hps + hh]
        fwd = (jnp.where(d >= 0, jnp.exp(jnp.maximum(d, 0.0) * lgf), 0.0),
               jnp.exp((ri + 1.0) * lgf), jnp.exp((c - 1.0 - ri) * lgf), jnp.exp(jnp.full((dk, dk), c * lgf, f32)))
        bwd = (jnp.where(d <= 0, jnp.exp(jnp.maximum(-d, 0.0) * lgb), 0.0),
               jnp.exp((c - ri) * lgb), jnp.exp(ri * lgb), jnp.exp(jnp.full((dk, dk), c * lgb, f32)))
        consts.append((fwd, bwd))

    def body(n, states):
        chains = []
        for hh in range(hps):
            for direction, (chunk_idx, o_scr) in enumerate(((n, of_scr), (n_chunks - 1 - n, ob_scr))):
                decay, qdec, kdec, cd = consts[hh][direction]
                r = pl.ds(pl.multiple_of(chunk_idx * c, c), c)
                cs = slice(hh * dk, (hh + 1) * dk)
                q = q_ref[r, cs]
                k = k_ref[r, cs] * kscale
                vb = v_ref[r, cs].astype(bf16)
                state = states[hh][direction]
                raw = _dot_nt(q.astype(bf16), k.astype(bf16))
                cross = _dot((q * qdec).astype(bf16), state.astype(bf16))
                kv = _dot((k * kdec).T.astype(bf16), vb)
                chains.append((raw, cross, kv, vb, decay, cd, state, o_scr, r, cs))
        scores = [(raw * decay).astype(bf16) for raw, _, _, _, decay, *_ in chains]
        new = []
        for (raw, cross, kv, vb, decay, cd, state, o_scr, r, cs), sc in zip(chains, scores):
            o_scr[r, cs] = _dot(sc, vb) + cross
            new.append(cd * state + kv)
        return tuple((new[2 * hh], new[2 * hh + 1]) for hh in range(hps))

    init = tuple((s0f_ref[0, hh], s0b_ref[0, hh]) for hh in range(hps))
    final = lax.fori_loop(0, n_chunks, body, init)
    for hh in range(hps):
        sf_ref[0, hh] = final[hh][0]
        sb_ref[0, hh] = final[hh][1]

    def hn(o):
        return o * lax.rsqrt(jnp.mean(o * o, axis=-1, keepdims=True) + NORM_EPS)

    def finish(n, carry):
        r = pl.ds(pl.multiple_of(n * c, c), c)
        for hh in range(hps):
            cs = slice(hh * dk, (hh + 1) * dk)
            gf = gf_ref[r, cs]
            gb = gb_ref[r, cs]
            out = hn(of_scr[r, cs]) * (gf * jax.nn.sigmoid(gf)) + hn(ob_scr[r, cs]) * (gb * jax.nn.sigmoid(gb))
            o_ref[r, cs] = out.astype(o_ref.dtype)
        return carry

    lax.fori_loop(0, n_chunks, finish, 0)


def retention(proj, log_gamma, s0f, s0b, nb, seq, col0, n_heads, dk):
    n_chunks = seq // RET_CHUNK
    hps = RET_HEADS_PER_STEP if n_heads % RET_HEADS_PER_STEP == 0 else 1
    w = hps * dk
    c0 = col0 // w
    gpb = n_heads // hps

    def col_spec(k):
        return pl.BlockSpec((seq, w), lambda b, h: (b, c0 + k * gpb + h))

    st_spec = pl.BlockSpec((1, hps, dk, dk), lambda b, h: (b, h, 0, 0))
    st_shape = jax.ShapeDtypeStruct((nb, n_heads, dk, dk), f32)
    return pl.pallas_call(
        functools.partial(_ret_kernel, n_chunks=n_chunks, dk=dk, hps=hps),
        grid=(nb, gpb),
        in_specs=[pl.BlockSpec(memory_space=pltpu.SMEM),
                  col_spec(0), col_spec(1), col_spec(2), col_spec(3), col_spec(4),
                  st_spec, st_spec],
        out_specs=[pl.BlockSpec((seq, w), lambda b, h: (b, h)), st_spec, st_spec],
        out_shape=[jax.ShapeDtypeStruct((nb * seq, n_heads * dk), bf16), st_shape, st_shape],
        scratch_shapes=[pltpu.VMEM((seq, w), f32), pltpu.VMEM((seq, w), f32)],
        compiler_params=_params("parallel", "parallel"),
        name="retention",
    )(log_gamma, proj, proj, proj, proj, proj, s0f, s0b)


def _merge_kernel(oa_ref, or_ref, wa_ref, wr_ref, ga_ref, gr_ref, o_ref):
    a = _dot(oa_ref[...], wa_ref[...])
    r = _dot(or_ref[...], wr_ref[...])
    o_ref[...] = (jax.nn.sigmoid(ga_ref[...]) * a + jax.nn.sigmoid(gr_ref[...]) * r).astype(o_ref.dtype)


def merge_branches(o_att, o_ret, w_pa, w_pr, proj, gate_col0, tm, tn):
    ntok, qw = o_att.shape
    rw = o_ret.shape[1]
    d = w_pa.shape[1]
    ga0 = gate_col0 // tn
    gr0 = (gate_col0 + d) // tn
    return pl.pallas_call(
        _merge_kernel,
        grid=(ntok // tm, d // tn),
        in_specs=[pl.BlockSpec((tm, qw), lambda i, j: (i, 0)),
                  pl.BlockSpec((tm, rw), lambda i, j: (i, 0)),
                  pl.BlockSpec((qw, tn), lambda i, j: (0, j)),
                  pl.BlockSpec((rw, tn), lambda i, j: (0, j)),
                  pl.BlockSpec((tm, tn), lambda i, j: (i, ga0 + j)),
                  pl.BlockSpec((tm, tn), lambda i, j: (i, gr0 + j))],
        out_specs=pl.BlockSpec((tm, tn), lambda i, j: (i, j)),
        out_shape=jax.ShapeDtypeStruct((ntok, d), bf16),
        compiler_params=_params("parallel", "arbitrary"),
        name="merge_branches",
    )(o_att, o_ret, w_pa, w_pr, proj, proj)


def _out_proj_kernel(m_ref, w_ref, x_ref, g_ref, o_ref):
    o_ref[...] = x_ref[...] + g_ref[0] * _dot(m_ref[...], w_ref[...])


def out_projection(merged, w_out, x, mod3, row_of_tile, k_gate, tm, tn):
    ntok, d = x.shape
    kd = merged.shape[1]
    return pl.pallas_call(
        _out_proj_kernel,
        grid=(ntok // tm, d // tn),
        in_specs=[pl.BlockSpec((tm, kd), lambda i, j: (i, 0)),
                  pl.BlockSpec((kd, tn), lambda i, j: (0, j)),
                  pl.BlockSpec((tm, tn), lambda i, j: (i, j)),
                  pl.BlockSpec((1, 1, tn), lambda i, j: (row_of_tile(i, tm) * 6 + k_gate, 0, j))],
        out_specs=pl.BlockSpec((tm, tn), lambda i, j: (i, j)),
        out_shape=jax.ShapeDtypeStruct((ntok, d), f32),
        compiler_params=_params("parallel", "arbitrary"),
        name="out_projection",
    )(merged, w_out, x, mod3)


def _extract_round(r, x_ref, rank_ref, vals_ref, pos, sentinel):
    x = x_ref[...]
    m = jnp.max(x, axis=0, keepdims=True)
    idx = jnp.min(jnp.where(x == m, pos, sentinel), axis=0, keepdims=True)
    sel = pos == idx
    rank_ref[...] = jnp.where(sel, lax.convert_element_type(r, f32), rank_ref[...])
    x_ref[...] = jnp.where(sel, -jnp.inf, x)
    vals_ref[pl.ds(r, 1), :] = m


def _candidate_layout(k):
    slabs = []
    for a in range(k // 2):
        n_valid = k // (a + 1)
        slabs.append((a, n_valid, -(-n_valid // SUBLANES) * SUBLANES))
    return slabs


def _peer_route_kernel(q_ref, keys_ref, n1_ref, e1_ref, r2_ref, e2_ref, x1_scr, x2_scr, v1_scr, v2_scr,
                       c_scr, crank_scr, cv_scr, rank2_scr, e2_scr, *, half):
    k = PEER_TOPK
    n_keys, t = x1_scr.shape
    s1 = _dot_nt(keys_ref[0, 0].astype(bf16), q_ref[:, :half].astype(bf16))
    s2 = _dot_nt(keys_ref[0, 1].astype(bf16), q_ref[:, half:].astype(bf16))
    x1_scr[...] = s1
    x2_scr[...] = s2
    e1_ref[0] = jnp.exp(s1 - jnp.max(s1, axis=0, keepdims=True))
    e2_scr[...] = jnp.exp(s2 - jnp.max(s2, axis=0, keepdims=True))
    n1_ref[0] = jnp.full((n_keys, t), float(k), f32)
    rank2_scr[...] = jnp.full((n_keys, t), float(k), f32)
    key_pos = lax.broadcasted_iota(jnp.int32, (n_keys, t), 0).astype(f32)

    def sub_round(r, carry):
        _extract_round(r, x1_scr, n1_ref.at[0], v1_scr, key_pos, float(n_keys))
        _extract_round(r, x2_scr, rank2_scr, v2_scr, key_pos, float(n_keys))
        return carry

    lax.fori_loop(0, k, sub_round, 0)

    v1 = v1_scr[...]
    v2 = v2_scr[...]
    slabs = _candidate_layout(k)
    cands, poss = [], []
    for a, n_valid, n_rows in slabs:
        b = lax.broadcasted_iota(jnp.int32, (n_rows, t), 0)
        cands.append(jnp.where(b < n_valid, v1[a:a + 1, :] + v2[:n_rows, :], -jnp.inf))
        poss.append((a * k + b).astype(f32))
    a_hi = lax.broadcasted_iota(jnp.int32, (k // 2, t), 0) + k // 2
    cands.append(v1[k // 2:, :] + v2[0:1, :])
    poss.append((a_hi * k).astype(f32))
    c_scr[...] = jnp.concatenate(cands, axis=0)
    cand_pos = jnp.concatenate(poss, axis=0)
    crank_scr[...] = jnp.full(c_scr.shape, float(k), f32)

    def cand_round(r, carry):
        _extract_round(r, c_scr, crank_scr, cv_scr, cand_pos, float(k * k))
        return carry

    lax.fori_loop(0, k, cand_round, 0)

    picked = jnp.where(crank_scr[...] < float(k), 1.0, 0.0)
    cnt = []
    row0 = 0
    for a, n_valid, n_rows in slabs:
        cnt.append(jnp.sum(picked[row0:row0 + n_rows, :], axis=0, keepdims=True))
        row0 += n_rows
    for a in range(k // 2, k):
        cnt.append(picked[row0 + a - k // 2:row0 + a - k // 2 + 1, :])
    cvals = cv_scr[...]
    z = jnp.sum(jnp.exp(cvals - cvals[0:1, :]), axis=0, keepdims=True)
    rank1 = n1_ref[0]
    n1 = jnp.zeros_like(rank1)
    for a in range(k):
        n1 = jnp.where(rank1 == float(a), cnt[a], n1)
    n1_ref[0] = n1
    r2_ref[0] = rank2_scr[...].astype(bf16)
    e2_ref[0] = (e2_scr[...] / z).astype(bf16)


def peer_route(qp, sub_keys, t_tile):
    ntok = qp.shape[0]
    heads, _, n_keys, half = sub_keys.shape
    k = PEER_TOPK
    n_cand = sum(s[2] for s in _candidate_layout(k)) + k // 2
    out_spec = pl.BlockSpec((1, n_keys, t_tile), lambda i, h: (h, 0, i))
    out_shape = jax.ShapeDtypeStruct((heads, n_keys, ntok), f32)
    out_shape_b = jax.ShapeDtypeStruct((heads, n_keys, ntok), bf16)
    return pl.pallas_call(
        functools.partial(_peer_route_kernel, half=half),
        grid=(ntok // t_tile, heads),
        in_specs=[pl.BlockSpec((t_tile, 2 * half), lambda i, h: (i, h)),
                  pl.BlockSpec((1, 2, n_keys, half), lambda i, h: (h, 0, 0, 0))],
        out_specs=[out_spec] * 4,
        out_shape=[out_shape, out_shape, out_shape_b, out_shape_b],
        scratch_shapes=[pltpu.VMEM((n_keys, t_tile), f32), pltpu.VMEM((n_keys, t_tile), f32),
                        pltpu.VMEM((k, t_tile), f32), pltpu.VMEM((k, t_tile), f32),
                        pltpu.VMEM((n_cand, t_tile), f32), pltpu.VMEM((n_cand, t_tile), f32),
                        pltpu.VMEM((k, t_tile), f32),
                        pltpu.VMEM((n_keys, t_tile), f32), pltpu.VMEM((n_keys, t_tile), f32)],
        compiler_params=_params("parallel", "parallel"),
        name="peer_route",
    )(qp, sub_keys)


BF16_SUBLANES = 2 * SUBLANES


def _bcast_rows_bf16(row, n_rows):
    tile = jnp.broadcast_to(row, (BF16_SUBLANES, row.shape[1])).astype(bf16)
    return jnp.concatenate([tile] * (n_rows // BF16_SUBLANES), axis=0)


def _peer_dense_kernel(ht_ref, u_ref, v_ref, n1_ref, e1_ref, r2_ref, e2_ref, acc_ref, coef_scr, r2_scr, e2_scr,
                       *, heads, n_keys):
    e = pl.program_id(1)
    et = u_ref.shape[0]
    t = ht_ref.shape[1]
    cpt = et // n_keys

    @pl.when(e == 0)
    def _():
        acc_ref[...] = jnp.zeros_like(acc_ref)
        r2_scr[...] = r2_ref[...]
        e2_scr[...] = e2_ref[...]

    act = _dot(u_ref[...], ht_ref[...])
    for c in range(cpt):
        i = e * cpt + c
        n1_rows = [n1_ref[h, pl.ds(i, 1), :] for h in range(heads)]
        e1_rows = [e1_ref[h, pl.ds(i, 1), :] for h in range(heads)]
        for tk in range(t // LANES):
            cols = slice(tk * LANES, (tk + 1) * LANES)
            gate = None
            for h in range(heads):
                n1 = _bcast_rows_bf16(n1_rows[h][:, cols], n_keys)
                e1 = _bcast_rows_bf16(e1_rows[h][:, cols], n_keys)
                term = jnp.where(r2_scr[h, :, cols] < n1, e2_scr[h, :, cols] * e1, jnp.zeros((), bf16))
                gate = term if gate is None else gate + term
            a = act[c * n_keys:(c + 1) * n_keys, cols].astype(bf16)
            gelu = 0.5 * a * (1.0 + lax.erf(a * (2.0 ** -0.5)))
            coef_scr[c, cols, :] = (gate * gelu).T
    lhs = jnp.concatenate([coef_scr[c] for c in range(cpt)], axis=-1)
    acc_ref[...] += _dot(lhs, v_ref[...])


def peer_dense(h_t, u, v, n1, e1, r2, e2, t_tile, e_tile):
    d, ntok = h_t.shape
    n_exp = u.shape[0]
    heads, n_keys, _ = n1.shape
    once = pl.Buffered(1)
    route_spec = pl.BlockSpec((heads, n_keys, t_tile), lambda t, e: (0, 0, t), pipeline_mode=once)
    return pl.pallas_call(
        functools.partial(_peer_dense_kernel, heads=heads, n_keys=n_keys),
        grid=(ntok // t_tile, n_exp // e_tile),
        in_specs=[pl.BlockSpec((d, t_tile), lambda t, e: (0, t), pipeline_mode=once),
                  pl.BlockSpec((e_tile, d), lambda t, e: (e, 0)),
                  pl.BlockSpec((e_tile, d), lambda t, e: (e, 0)),
                  route_spec, route_spec, route_spec, route_spec],
        out_specs=pl.BlockSpec((t_tile, d), lambda t, e: (t, 0)),
        out_shape=jax.ShapeDtypeStruct((ntok, d), f32),
        scratch_shapes=[pltpu.VMEM((e_tile // n_keys, t_tile, n_keys), bf16),
                        pltpu.VMEM((heads, n_keys, t_tile), bf16),
                        pltpu.VMEM((heads, n_keys, t_tile), bf16)],
        compiler_params=_params("parallel", "arbitrary"),
        name="peer_dense",
    )(h_t, u, v, n1, e1, r2, e2)


def _residual_kernel(x_ref, p_ref, g_ref, o_ref):
    o_ref[...] = x_ref[...] + g_ref[0] * p_ref[...]


def _residual_norm_kernel(x_ref, p_ref, g_ref, w_ref, o_ref):
    x = x_ref[...] + g_ref[0] * p_ref[...]
    y = x * lax.rsqrt(jnp.mean(x * x, axis=-1, keepdims=True) + NORM_EPS)
    o_ref[...] = y * w_ref[...]


def gated_residual(x, p, mod3, row_of_tile, k_gate, tm, norm_w=None):
    ntok, d = x.shape
    row_spec = pl.BlockSpec((tm, d), lambda i: (i, 0))
    in_specs = [row_spec, row_spec, pl.BlockSpec((1, 1, d), lambda i: (row_of_tile(i, tm) * 6 + k_gate, 0, 0))]
    args = [x, p, mod3]
    kern = _residual_kernel
    if norm_w is not None:
        in_specs.append(pl.BlockSpec((1, d), lambda i: (0, 0)))
        args.append(norm_w.reshape(1, d))
        kern = _residual_norm_kernel
    return pl.pallas_call(
        kern,
        grid=(ntok // tm,),
        in_specs=in_specs,
        out_specs=row_spec,
        out_shape=jax.ShapeDtypeStruct((ntok, d), f32),
        compiler_params=_params("parallel"),
        name="gated_residual",
    )(*args)


def _axial_rope_tables(n_tokens, hd):
    rows = n_tokens // GRID_W
    row = jnp.repeat(jnp.arange(rows), GRID_W).astype(f32)
    col = jnp.tile(jnp.arange(GRID_W), rows).astype(f32)
    half = hd // 2
    inv = 1.0 / (ROPE_BASE ** (jnp.arange(0, half, 2, dtype=f32) / half))
    ar = row[:, None] * inv[None, :]
    ac = col[:, None] * inv[None, :]
    ang = jnp.concatenate([ar, ar, ac, ac], axis=-1)
    return jnp.cos(ang), jnp.sin(ang)


def kernel(x_prompt, x_sample, cache_attn_k, cache_attn_v, state_ret_fwd, state_ret_bwd, c, c_ctx, w_ada, b_ada,
           norm_mix, norm_ffn, w_in, attn_sink, ret_decay_logit, w_proj_attn, w_proj_ret, w_out, peer_w_query,
           peer_sub_keys, peer_u, peer_v, norm_final):
    batch, seq, d = x_prompt.shape
    dbatch, dseq, _ = x_sample.shape
    depth = w_in.shape[0]
    assert depth >= 1
    n_heads = attn_sink.shape[1]
    _, _, past, n_kv, hd = cache_attn_k.shape
    ret_heads, ret_dk, ret_dv = state_ret_fwd.shape[2:]
    assert ret_dk == ret_dv
    peer_heads, peer_dkey = peer_w_query.shape[2:]
    q_w = n_heads * hd
    kv_w = n_kv * hd
    r_w = ret_heads * ret_dk
    ret_col0 = q_w + 2 * kv_w
    gate_col0 = ret_col0 + 5 * r_w

    n_prompt = batch * seq
    n_lat = dbatch * dseq

    def cond_row_ctx(i, tm):
        return 0

    def cond_row_lat(i, tm):
        return 1 + i // (dseq // tm)

    n_cond = -(-(dbatch + 1) // 8) * 8
    cond = jnp.zeros((n_cond, d), f32).at[0].set(c_ctx).at[1:dbatch + 1].set(c)
    cos, sin = _axial_rope_tables(dseq, hd)
    tm_small = _tile(512, n_prompt, dseq)
    tm_big = _tile(1024, n_prompt, dseq)
    tm_res = _tile(256, n_prompt, dseq)
    tn_in = _tile(1024, w_in.shape[2])
    tn_gate = _tile(512, d, gate_col0)
    tn_q = _tile(1024, peer_heads * peer_dkey)
    e_tile = _tile(512, peer_u.shape[1])

    xp = x_prompt.reshape(n_prompt, d)
    xs = x_sample.reshape(n_lat, d)
    new_k, new_v, new_sf, new_sb = [], [], [], []
    for l in range(depth):
        last = l == depth - 1
        mod3 = ada_modulation(cond, w_ada[l], b_ada[l]).reshape(n_cond * 6, 1, d)
        sink = attn_sink[l].astype(f32)
        log_gamma = jax.nn.log_sigmoid(ret_decay_logit[l].astype(f32))
        w_in_b = w_in[l].astype(bf16)
        w_pa_b = w_proj_attn[l].astype(bf16)
        w_pr_b = w_proj_ret[l].astype(bf16)
        w_out_b = w_out[l].astype(bf16)
        wq_b = peer_w_query[l].reshape(d, peer_heads * peer_dkey).astype(bf16)
        u_b = peer_u[l].astype(bf16)
        v_b = peer_v[l].astype(bf16)

        def channel_mix(x, proj, o_att, o_ret, cond_row):
            merged = merge_branches(o_att, o_ret, w_pa_b, w_pr_b, proj, gate_col0, tm_big, tn_gate)
            x = out_projection(merged, w_out_b, x, mod3, cond_row, 2, tm_big, tn_gate)
            qp, h2 = norm_mod_matmul(x, norm_ffn[l], mod3, cond_row, 3, 4, wq_b, tm_small, tn_q, True)
            n1, e1, r2, e2 = peer_route(qp, peer_sub_keys[l], _tile(512, x.shape[0]))
            p = peer_dense(h2.T, u_b, v_b, n1, e1, r2, e2, _tile(512, x.shape[0]), e_tile)
            return gated_residual(x, p, mod3, cond_row, 5, tm_res, norm_final if last else None)

        proj = norm_mod_matmul(xp, norm_mix[l], mod3, cond_row_ctx, 0, 1, w_in_b, tm_small, tn_in, False)
        o_att = context_attention(proj, sink, batch, seq, n_heads, n_kv, hd)
        zero_state = jnp.zeros((batch, ret_heads, ret_dk, ret_dv), f32)
        o_ret, s_f, s_b = retention(proj, log_gamma, zero_state, zero_state, batch, seq, ret_col0, ret_heads, ret_dk)
        new_k.append(proj[:, q_w:q_w + kv_w].reshape(batch, seq, n_kv, hd))
        new_v.append(proj[:, q_w + kv_w:q_w + 2 * kv_w].reshape(batch, seq, n_kv, hd))
        new_sf.append(s_f)
        new_sb.append(s_b)
        xp = channel_mix(xp, proj, o_att, o_ret, cond_row_ctx)

        proj = norm_mod_matmul(xs, norm_mix[l], mod3, cond_row_lat, 0, 1, w_in_b, tm_small, tn_in, False)
        o_att = latent_attention(proj, sink, cache_attn_k[:, l].reshape(dbatch, past, kv_w),
                                 cache_attn_v[:, l].reshape(dbatch, past, kv_w), cos, sin,
                                 dbatch, dseq, n_heads, n_kv, hd)
        o_ret, _, _ = retention(proj, log_gamma, state_ret_fwd[:, l], state_ret_bwd[:, l], dbatch, dseq,
                                ret_col0, ret_heads, ret_dk)
        xs = channel_mix(xs, proj, o_att, o_ret, cond_row_lat)

    return (xp.reshape(batch, seq, d), xs.reshape(dbatch, dseq, d), jnp.stack(new_k, axis=1),
            jnp.stack(new_v, axis=1), jnp.stack(new_sf, axis=1), jnp.stack(new_sb, axis=1))
```

```python
import functools

import jax
import jax.numpy as jnp
from jax import lax
from jax.experimental import pallas as pl
from jax.experimental.pallas import tpu as pltpu

GRID_W = 64
WINDOW = 128
ATT_BLOCK = 128
ROPE_BASE = 10000.0
RET_CHUNK = 128
PEER_TOPK = 16
NORM_EPS = 1e-6

LANES = 128
SUBLANES = 8
MXU_DIM = 256
VMEM_LIMIT_BYTES = 56 * 1024 * 1024
NEG_BIG = -1e30

f32 = jnp.float32
bf16 = jnp.bfloat16


def _params(*sem):
    return pltpu.CompilerParams(dimension_semantics=sem, vmem_limit_bytes=VMEM_LIMIT_BYTES)


def _tile(pref, *dims):
    t = pref
    while any(dim % t for dim in dims):
        t //= 2
    assert t >= LANES
    return t


def _dot(a, b):
    return jnp.dot(a, b, preferred_element_type=f32)


def _dot_nt(a, b):
    return lax.dot_general(a, b, (((1,), (1,)), ((), ())), preferred_element_type=f32)


def _dot_tn(a, b):
    return lax.dot_general(a, b, (((0,), (0,)), ((), ())), preferred_element_type=f32)


def _ada_kernel(cond_ref, w_ref, b_ref, o_ref):
    c = cond_ref[...]
    s = (c * jax.nn.sigmoid(c)).astype(bf16)
    o_ref[...] = _dot(s, w_ref[...].astype(bf16)) + b_ref[...]


def ada_modulation(cond, w_ada, b_ada):
    r, d = cond.shape
    n = w_ada.shape[1]
    tn = _tile(512, n)
    return pl.pallas_call(
        _ada_kernel,
        grid=(n // tn,),
        in_specs=[pl.BlockSpec((r, d), lambda j: (0, 0)),
                  pl.BlockSpec((d, tn), lambda j: (0, j)),
                  pl.BlockSpec((1, tn), lambda j: (0, j))],
        out_specs=pl.BlockSpec((r, tn), lambda j: (0, j)),
        out_shape=jax.ShapeDtypeStruct((r, n), f32),
        compiler_params=_params("parallel"),
        name="ada_modulation",
    )(cond, w_ada, b_ada.reshape(1, n))


def _norm_mod(x, g, sh, sc):
    y = x * lax.rsqrt(jnp.mean(x * x, axis=-1, keepdims=True) + NORM_EPS)
    y = y * g
    return y * (1.0 + sc) + sh


def _norm_mod_matmul_kernel(x_ref, g_ref, sh_ref, sc_ref, w_ref, o_ref, h_scr):
    @pl.when(pl.program_id(1) == 0)
    def _():
        h_scr[...] = _norm_mod(x_ref[...], g_ref[...], sh_ref[0], sc_ref[0]).astype(bf16)

    o_ref[...] = _dot(h_scr[...], w_ref[...])


def _norm_mod_matmul_h_kernel(x_ref, g_ref, sh_ref, sc_ref, w_ref, o_ref, h_ref):
    @pl.when(pl.program_id(1) == 0)
    def _():
        h_ref[...] = _norm_mod(x_ref[...], g_ref[...], sh_ref[0], sc_ref[0]).astype(bf16)

    o_ref[...] = _dot(h_ref[...], w_ref[...])


def norm_mod_matmul(x, g, mod3, row_of_tile, k_shift, k_scale, w, tm, tn, emit_h):
    ntok, d = x.shape
    n = w.shape[1]

    def mod_spec(k):
        return pl.BlockSpec((1, 1, d), lambda i, j: (row_of_tile(i, tm) * 6 + k, 0, 0))

    in_specs = [pl.BlockSpec((tm, d), lambda i, j: (i, 0)),
                pl.BlockSpec((1, d), lambda i, j: (0, 0)),
                mod_spec(k_shift), mod_spec(k_scale),
                pl.BlockSpec((d, tn), lambda i, j: (0, j))]
    o_spec = pl.BlockSpec((tm, tn), lambda i, j: (i, j))
    if emit_h:
        return pl.pallas_call(
            _norm_mod_matmul_h_kernel,
            grid=(ntok // tm, n // tn),
            in_specs=in_specs,
            out_specs=[o_spec, pl.BlockSpec((tm, d), lambda i, j: (i, 0))],
            out_shape=[jax.ShapeDtypeStruct((ntok, n), f32), jax.ShapeDtypeStruct((ntok, d), bf16)],
            compiler_params=_params("parallel", "arbitrary"),
            name="norm_mod_matmul_h",
        )(x, g.reshape(1, d), mod3, mod3, w)
    return pl.pallas_call(
        _norm_mod_matmul_kernel,
        grid=(ntok // tm, n // tn),
        in_specs=in_specs,
        out_specs=o_spec,
        out_shape=jax.ShapeDtypeStruct((ntok, n), f32),
        scratch_shapes=[pltpu.VMEM((tm, d), bf16)],
        compiler_params=_params("parallel", "arbitrary"),
        name="norm_mod_matmul",
    )(x, g.reshape(1, d), mod3, mod3, w)


def _softmax_pv(logits, values, sink):
    m = sink
    for lg in logits:
        m = jnp.maximum(m, jnp.max(lg, axis=-1, keepdims=True))
    den = jnp.exp(sink - m)
    acc = None
    for lg, v in zip(logits, values):
        p = jnp.exp(lg - m)
        den = den + jnp.sum(p, axis=-1, keepdims=True)
        pv = _dot(p.astype(bf16), v)
        acc = pv if acc is None else acc + pv
    return acc / den


def _sink_column(sink_ref, kvh, group, rows):
    return jnp.concatenate([jnp.full((rows, 1), sink_ref[kvh * group + g], f32) for g in range(group)], axis=0)


def _ctx_attn_kernel(sink_ref, q_ref, k_ref, v_ref, o_ref, *, group, hd):
    kvh = pl.program_id(1)
    scale = hd ** -0.5
    k = k_ref[...].astype(bf16)
    v = v_ref[...].astype(bf16)
    for g in range(group):
        q = q_ref[:, g * hd:(g + 1) * hd].astype(bf16)
        lg = _dot_nt(q, k) * scale
        o = _softmax_pv([lg], [v], sink_ref[kvh * group + g])
        o_ref[:, g * hd:(g + 1) * hd] = o.astype(o_ref.dtype)


def context_attention(proj, sink, nb, seq, n_heads, n_kv, hd):
    group = n_heads // n_kv
    q_w = n_heads * hd
    kcol = q_w // hd
    vcol = (q_w + n_kv * hd) // hd
    return pl.pallas_call(
        functools.partial(_ctx_attn_kernel, group=group, hd=hd),
        grid=(nb, n_kv),
        in_specs=[pl.BlockSpec(memory_space=pltpu.SMEM),
                  pl.BlockSpec((seq, group * hd), lambda b, h: (b, h)),
                  pl.BlockSpec((seq, hd), lambda b, h: (b, kcol + h)),
                  pl.BlockSpec((seq, hd), lambda b, h: (b, vcol + h))],
        out_specs=pl.BlockSpec((seq, group * hd), lambda b, h: (b, h)),
        out_shape=jax.ShapeDtypeStruct((nb * seq, q_w), bf16),
        compiler_params=_params("parallel", "parallel"),
        name="context_attention",
    )(sink, proj, proj, proj)


def _rope(x, cos, sin, first_half):
    rot = jnp.where(first_half, -pltpu.roll(x, 3 * (LANES // 4), 1), pltpu.roll(x, LANES // 4, 1))
    return x * cos + rot * sin


def _lat_attn_kernel(sink_ref, q_ref, kp_ref, kc_ref, kn_ref, vp_ref, vc_ref, vn_ref, kx_ref, vx_ref,
                     cp_ref, cc_ref, cn_ref, sp_ref, sc_ref, sn_ref, o_ref, *, group, hd):
    n = pl.program_id(1)
    kvh = pl.program_id(2)
    nblk = pl.num_programs(1)
    scale = hd ** -0.5
    blk = (ATT_BLOCK, hd)
    lane = lax.broadcasted_iota(jnp.int32, blk, 1)
    first_half = (lane % (hd // 2)) < (hd // 4)
    row = lax.broadcasted_iota(jnp.int32, (group * ATT_BLOCK, ATT_BLOCK), 0) % ATT_BLOCK
    col = lax.broadcasted_iota(jnp.int32, (group * ATT_BLOCK, ATT_BLOCK), 1)
    mask_p = jnp.logical_and(col >= row, n > 0)
    mask_n = jnp.logical_and(col <= row, n < nblk - 1)
    kp = _rope(kp_ref[...], cp_ref[...], sp_ref[...], first_half).astype(bf16)
    kc = _rope(kc_ref[...], cc_ref[...], sc_ref[...], first_half).astype(bf16)
    kn = _rope(kn_ref[...], cn_ref[...], sn_ref[...], first_half).astype(bf16)
    kx = kx_ref[0].astype(bf16)
    values = [vp_ref[...].astype(bf16), vc_ref[...].astype(bf16), vn_ref[...].astype(bf16), vx_ref[0].astype(bf16)]
    q = jnp.concatenate([_rope(q_ref[:, g * hd:(g + 1) * hd], cc_ref[...], sc_ref[...], first_half).astype(bf16)
                         for g in range(group)], axis=0)
    lp = jnp.where(mask_p, _dot_nt(q, kp) * scale, NEG_BIG)
    lc = _dot_nt(q, kc) * scale
    ln = jnp.where(mask_n, _dot_nt(q, kn) * scale, NEG_BIG)
    lx = _dot_nt(q, kx) * scale
    o = _softmax_pv([lp, lc, ln, lx], values, _sink_column(sink_ref, kvh, group, ATT_BLOCK))
    for g in range(group):
        o_ref[:, g * hd:(g + 1) * hd] = o[g * ATT_BLOCK:(g + 1) * ATT_BLOCK].astype(o_ref.dtype)


def latent_attention(proj, sink, k_ctx, v_ctx, cos, sin, nb, seq, n_heads, n_kv, hd):
    assert WINDOW == ATT_BLOCK
    group = n_heads // n_kv
    q_w = n_heads * hd
    kcol = q_w // hd
    vcol = (q_w + n_kv * hd) // hd
    nblk = seq // ATT_BLOCK
    past = k_ctx.shape[1]

    def rb(b, n):
        return b * nblk + n

    def prev(n):
        return jnp.maximum(n - 1, 0)

    def nxt(n):
        return jnp.minimum(n + 1, nblk - 1)

    def kv_spec(col0, f):
        return pl.BlockSpec((ATT_BLOCK, hd), lambda b, n, h: (rb(b, f(n)), col0 + h))

    def tab_spec(f):
        return pl.BlockSpec((ATT_BLOCK, hd), lambda b, n, h: (f(n), 0))

    ident = lambda n: n
    ctx_spec = pl.BlockSpec((1, past, hd), lambda b, n, h: (b, 0, h))
    return pl.pallas_call(
        functools.partial(_lat_attn_kernel, group=group, hd=hd),
        grid=(nb, nblk, n_kv),
        in_specs=[pl.BlockSpec(memory_space=pltpu.SMEM),
                  pl.BlockSpec((ATT_BLOCK, group * hd), lambda b, n, h: (rb(b, n), h)),
                  kv_spec(kcol, prev), kv_spec(kcol, ident), kv_spec(kcol, nxt),
                  kv_spec(vcol, prev), kv_spec(vcol, ident), kv_spec(vcol, nxt),
                  ctx_spec, ctx_spec,
                  tab_spec(prev), tab_spec(ident), tab_spec(nxt),
                  tab_spec(prev), tab_spec(ident), tab_spec(nxt)],
        out_specs=pl.BlockSpec((ATT_BLOCK, group * hd), lambda b, n, h: (rb(b, n), h)),
        out_shape=jax.ShapeDtypeStruct((nb * seq, q_w), bf16),
        compiler_params=_params("parallel", "parallel", "parallel"),
        name="latent_attention",
    )(sink, proj, proj, proj, proj, proj, proj, proj, k_ctx, v_ctx, cos, cos, cos, sin, sin, sin)


RET_HEADS_PER_STEP = 2


def _ret_kernel(lg_ref, q_ref, k_ref, v_ref, gf_ref, gb_ref, s0f_ref, s0b_ref, o_ref, sf_ref, sb_ref,
                of_scr, ob_scr, *, n_chunks, dk, hps):
    hb = pl.program_id(1)
    c = RET_CHUNK
    ii = lax.broadcasted_iota(jnp.int32, (c, c), 0).astype(f32)
    jj = lax.broadcasted_iota(jnp.int32, (c, c), 1).astype(f32)
    d = ii - jj
    ri = lax.broadcasted_iota(jnp.int32, (c, dk), 0).astype(f32)
    kscale = dk ** -0.5

    consts = []
    for hh in range(hps):
        lgf = lg_ref[0, hb * hps + hh]
        lgb = lg_ref[1, hb * hps + hh]
        fwd = (jnp.where(d >= 0, jnp.exp(jnp.maximum(d, 0.0) * lgf), 0.0),
               jnp.exp((ri + 1.0) * lgf), jnp.exp((c - 1.0 - ri) * lgf), jnp.exp(jnp.full((dk, dk), c * lgf, f32)))
        bwd = (jnp.where(d <= 0, jnp.exp(jnp.maximum(-d, 0.0) * lgb), 0.0),
               jnp.exp((c - ri) * lgb), jnp.exp(ri * lgb), jnp.exp(jnp.full((dk, dk), c * lgb, f32)))
        consts.append((fwd, bwd))

    def body(n, states):
        chains = []
        for hh in range(hps):
            for direction, (chunk_idx, o_scr) in enumerate(((n, of_scr), (n_chunks - 1 - n, ob_scr))):
                decay, qdec, kdec, cd = consts[hh][direction]
                r = pl.ds(pl.multiple_of(chunk_idx * c, c), c)
                cs = slice(hh * dk, (hh + 1) * dk)
                q = q_ref[r, cs]
                k = k_ref[r, cs] * kscale
                vb = v_ref[r, cs].astype(bf16)
                state = states[hh][direction]
                raw = _dot_nt(q.astype(bf16), k.astype(bf16))
                cross = _dot((q * qdec).astype(bf16), state.astype(bf16))
                kv = _dot((k * kdec).T.astype(bf16), vb)
                chains.append((raw, cross, kv, vb, decay, cd, state, o_scr, r, cs))
        scores = [(raw * decay).astype(bf16) for raw, _, _, _, decay, *_ in chains]
        new = []
        for (raw, cross, kv, vb, decay, cd, state, o_scr, r, cs), sc in zip(chains, scores):
            o_scr[r, cs] = _dot(sc, vb) + cross
            new.append(cd * state + kv)
        return tuple((new[2 * hh], new[2 * hh + 1]) for hh in range(hps))

    init = tuple((s0f_ref[0, hh], s0b_ref[0, hh]) for hh in range(hps))
    final = lax.fori_loop(0, n_chunks, body, init)
    for hh in range(hps):
        sf_ref[0, hh] = final[hh][0]
        sb_ref[0, hh] = final[hh][1]

    def hn(o):
        return o * lax.rsqrt(jnp.mean(o * o, axis=-1, keepdims=True) + NORM_EPS)

    def finish(n, carry):
        r = pl.ds(pl.multiple_of(n * c, c), c)
        for hh in range(hps):
            cs = slice(hh * dk, (hh + 1) * dk)
            gf = gf_ref[r, cs]
            gb = gb_ref[r, cs]
            out = hn(of_scr[r, cs]) * (gf * jax.nn.sigmoid(gf)) + hn(ob_scr[r, cs]) * (gb * jax.nn.sigmoid(gb))
            o_ref[r, cs] = out.astype(o_ref.dtype)
        return carry

    lax.fori_loop(0, n_chunks, finish, 0)


def retention(proj, log_gamma, s0f, s0b, nb, seq, col0, n_heads, dk):
    n_chunks = seq // RET_CHUNK
    hps = RET_HEADS_PER_STEP if n_heads % RET_HEADS_PER_STEP == 0 else 1
    w = hps * dk
    c0 = col0 // w
    gpb = n_heads // hps

    def col_spec(k):
        return pl.BlockSpec((seq, w), lambda b, h: (b, c0 + k * gpb + h))

    st_spec = pl.BlockSpec((1, hps, dk, dk), lambda b, h: (b, h, 0, 0))
    st_shape = jax.ShapeDtypeStruct((nb, n_heads, dk, dk), f32)
    return pl.pallas_call(
        functools.partial(_ret_kernel, n_chunks=n_chunks, dk=dk, hps=hps),
        grid=(nb, gpb),
        in_specs=[pl.BlockSpec(memory_space=pltpu.SMEM),
                  col_spec(0), col_spec(1), col_spec(2), col_spec(3), col_spec(4),
                  st_spec, st_spec],
        out_specs=[pl.BlockSpec((seq, w), lambda b, h: (b, h)), st_spec, st_spec],
        out_shape=[jax.ShapeDtypeStruct((nb * seq, n_heads * dk), bf16), st_shape, st_shape],
        scratch_shapes=[pltpu.VMEM((seq, w), f32), pltpu.VMEM((seq, w), f32)],
        compiler_params=_params("parallel", "parallel"),
        name="retention",
    )(log_gamma, proj, proj, proj, proj, proj, s0f, s0b)


def _merge_kernel(oa_ref, or_ref, wa_ref, wr_ref, ga_ref, gr_ref, o_ref):
    a = _dot(oa_ref[...], wa_ref[...])
    r = _dot(or_ref[...], wr_ref[...])
    o_ref[...] = (jax.nn.sigmoid(ga_ref[...]) * a + jax.nn.sigmoid(gr_ref[...]) * r).astype(o_ref.dtype)


def merge_branches(o_att, o_ret, w_pa, w_pr, proj, gate_col0, tm, tn):
    ntok, qw = o_att.shape
    rw = o_ret.shape[1]
    d = w_pa.shape[1]
    ga0 = gate_col0 // tn
    gr0 = (gate_col0 + d) // tn
    return pl.pallas_call(
        _merge_kernel,
        grid=(ntok // tm, d // tn),
        in_specs=[pl.BlockSpec((tm, qw), lambda i, j: (i, 0)),
                  pl.BlockSpec((tm, rw), lambda i, j: (i, 0)),
                  pl.BlockSpec((qw, tn), lambda i, j: (0, j)),
                  pl.BlockSpec((rw, tn), lambda i, j: (0, j)),
                  pl.BlockSpec((tm, tn), lambda i, j: (i, ga0 + j)),
                  pl.BlockSpec((tm, tn), lambda i, j: (i, gr0 + j))],
        out_specs=pl.BlockSpec((tm, tn), lambda i, j: (i, j)),
        out_shape=jax.ShapeDtypeStruct((ntok, d), bf16),
        compiler_params=_params("parallel", "arbitrary"),
        name="merge_branches",
    )(o_att, o_ret, w_pa, w_pr, proj, proj)


def _out_proj_kernel(m_ref, w_ref, x_ref, g_ref, o_ref):
    o_ref[...] = x_ref[...] + g_ref[0] * _dot(m_ref[...], w_ref[...])


def out_projection(merged, w_out, x, mod3, row_of_tile, k_gate, tm, tn):
    ntok, d = x.shape
    kd = merged.shape[1]
    return pl.pallas_call(
        _out_proj_kernel,
        grid=(ntok // tm, d // tn),
        in_specs=[pl.BlockSpec((tm, kd), lambda i, j: (i, 0)),
                  pl.BlockSpec((kd, tn), lambda i, j: (0, j)),
                  pl.BlockSpec((tm, tn), lambda i, j: (i, j)),
                  pl.BlockSpec((1, 1, tn), lambda i, j: (row_of_tile(i, tm) * 6 + k_gate, 0, j))],
        out_specs=pl.BlockSpec((tm, tn), lambda i, j: (i, j)),
        out_shape=jax.ShapeDtypeStruct((ntok, d), f32),
        compiler_params=_params("parallel", "arbitrary"),
        name="out_projection",
    )(merged, w_out, x, mod3)


def _extract_round(r, x_ref, rank_ref, vals_ref, pos, sentinel):
    x = x_ref[...]
    m = jnp.max(x, axis=0, keepdims=True)
    idx = jnp.min(jnp.where(x == m, pos, sentinel), axis=0, keepdims=True)
    sel = pos == idx
    rank_ref[...] = jnp.where(sel, lax.convert_element_type(r, f32), rank_ref[...])
    x_ref[...] = jnp.where(sel, -jnp.inf, x)
    vals_ref[pl.ds(r, 1), :] = m


def _candidate_layout(k):
    slabs = []
    for a in range(k // 2):
        n_valid = k // (a + 1)
        slabs.append((a, n_valid, -(-n_valid // SUBLANES) * SUBLANES))
    return slabs


def _peer_route_kernel(q_ref, keys_ref, n1_ref, e1_ref, r2_ref, e2_ref, x1_scr, x2_scr, v1_scr, v2_scr,
                       c_scr, crank_scr, cv_scr, rank2_scr, e2_scr, *, half):
    k = PEER_TOPK
    n_keys, t = x1_scr.shape
    s1 = _dot_nt(keys_ref[0, 0].astype(bf16), q_ref[:, :half].astype(bf16))
    s2 = _dot_nt(keys_ref[0, 1].astype(bf16), q_ref[:, half:].astype(bf16))
    x1_scr[...] = s1
    x2_scr[...] = s2
    e1_ref[0] = jnp.exp(s1 - jnp.max(s1, axis=0, keepdims=True))
    e2_scr[...] = jnp.exp(s2 - jnp.max(s2, axis=0, keepdims=True))
    n1_ref[0] = jnp.full((n_keys, t), float(k), f32)
    rank2_scr[...] = jnp.full((n_keys, t), float(k), f32)
    key_pos = lax.broadcasted_iota(jnp.int32, (n_keys, t), 0).astype(f32)

    def sub_round(r, carry):
        _extract_round(r, x1_scr, n1_ref.at[0], v1_scr, key_pos, float(n_keys))
        _extract_round(r, x2_scr, rank2_scr, v2_scr, key_pos, float(n_keys))
        return carry

    lax.fori_loop(0, k, sub_round, 0)

    v1 = v1_scr[...]
    v2 = v2_scr[...]
    slabs = _candidate_layout(k)
    cands, poss = [], []
    for a, n_valid, n_rows in slabs:
        b = lax.broadcasted_iota(jnp.int32, (n_rows, t), 0)
        cands.append(jnp.where(b < n_valid, v1[a:a + 1, :] + v2[:n_rows, :], -jnp.inf))
        poss.append((a * k + b).astype(f32))
    a_hi = lax.broadcasted_iota(jnp.int32, (k // 2, t), 0) + k // 2
    cands.append(v1[k // 2:, :] + v2[0:1, :])
    poss.append((a_hi * k).astype(f32))
    c_scr[...] = jnp.concatenate(cands, axis=0)
    cand_pos = jnp.concatenate(poss, axis=0)
    crank_scr[...] = jnp.full(c_scr.shape, float(k), f32)

    def cand_round(r, carry):
        _extract_round(r, c_scr, crank_scr, cv_scr, cand_pos, float(k * k))
        return carry

    lax.fori_loop(0, k, cand_round, 0)

    picked = jnp.where(crank_scr[...] < float(k), 1.0, 0.0)
    cnt = []
    row0 = 0
    for a, n_valid, n_rows in slabs:
        cnt.append(jnp.sum(picked[row0:row0 + n_rows, :], axis=0, keepdims=True))
        row0 += n_rows
    for a in range(k // 2, k):
        cnt.append(picked[row0 + a - k // 2:row0 + a - k // 2 + 1, :])
    cvals = cv_scr[...]
    z = jnp.sum(jnp.exp(cvals - cvals[0:1, :]), axis=0, keepdims=True)
    rank1 = n1_ref[0]
    n1 = jnp.zeros_like(rank1)
    for a in range(k):
        n1 = jnp.where(rank1 == float(a), cnt[a], n1)
    n1_ref[0] = n1
    r2_ref[0] = rank2_scr[...].astype(bf16)
    e2_ref[0] = (e2_scr[...] / z).astype(bf16)


def peer_route(qp, sub_keys, t_tile):
    ntok = qp.shape[0]
    heads, _, n_keys, half = sub_keys.shape
    k = PEER_TOPK
    n_cand = sum(s[2] for s in _candidate_layout(k)) + k // 2
    out_spec = pl.BlockSpec((1, n_keys, t_tile), lambda i, h: (h, 0, i))
    out_shape = jax.ShapeDtypeStruct((heads, n_keys, ntok), f32)
    out_shape_b = jax.ShapeDtypeStruct((heads, n_keys, ntok), bf16)
    return pl.pallas_call(
        functools.partial(_peer_route_kernel, half=half),
        grid=(ntok // t_tile, heads),
        in_specs=[pl.BlockSpec((t_tile, 2 * half), lambda i, h: (i, h)),
                  pl.BlockSpec((1, 2, n_keys, half), lambda i, h: (h, 0, 0, 0))],
        out_specs=[out_spec] * 4,
        out_shape=[out_shape, out_shape, out_shape_b, out_shape_b],
        scratch_shapes=[pltpu.VMEM((n_keys, t_tile), f32), pltpu.VMEM((n_keys, t_tile), f32),
                        pltpu.VMEM((k, t_tile), f32), pltpu.VMEM((k, t_tile), f32),
                        pltpu.VMEM((n_cand, t_tile), f32), pltpu.VMEM((n_cand, t_tile), f32),
                        pltpu.VMEM((k, t_tile), f32),
                        pltpu.VMEM((n_keys, t_tile), f32), pltpu.VMEM((n_keys, t_tile), f32)],
        compiler_params=_params("parallel", "parallel"),
        name="peer_route",
    )(qp, sub_keys)


BF16_SUBLANES = 2 * SUBLANES


def _bcast_rows_bf16(row, n_rows):
    tile = jnp.broadcast_to(row, (BF16_SUBLANES, row.shape[1])).astype(bf16)
    return jnp.concatenate([tile] * (n_rows // BF16_SUBLANES), axis=0)


def _peer_dense_kernel(ht_ref, u_ref, v_ref, n1_ref, e1_ref, r2_ref, e2_ref, acc_ref, coef_scr, r2_scr, e2_scr,
                       *, heads, n_keys):
    e = pl.program_id(1)
    et = u_ref.shape[0]
    t = ht_ref.shape[1]
    cpt = et // n_keys

    @pl.when(e == 0)
    def _():
        acc_ref[...] = jnp.zeros_like(acc_ref)
        r2_scr[...] = r2_ref[...]
        e2_scr[...] = e2_ref[...]

    act = _dot(u_ref[...], ht_ref[...])
    for c in range(cpt):
        i = e * cpt + c
        n1_rows = [n1_ref[h, pl.ds(i, 1), :] for h in range(heads)]
        e1_rows = [e1_ref[h, pl.ds(i, 1), :] for h in range(heads)]
        for tk in range(t // LANES):
            cols = slice(tk * LANES, (tk + 1) * LANES)
            gate = None
            for h in range(heads):
                n1 = _bcast_rows_bf16(n1_rows[h][:, cols], n_keys)
                e1 = _bcast_rows_bf16(e1_rows[h][:, cols], n_keys)
                term = jnp.where(r2_scr[h, :, cols] < n1, e2_scr[h, :, cols] * e1, jnp.zeros((), bf16))
                gate = term if gate is None else gate + term
            a = act[c * n_keys:(c + 1) * n_keys, cols].astype(bf16)
            gelu = 0.5 * a * (1.0 + lax.erf(a * (2.0 ** -0.5)))
            coef_scr[c, cols, :] = (gate * gelu).T
    lhs = jnp.concatenate([coef_scr[c] for c in range(cpt)], axis=-1)
    acc_ref[...] += _dot(lhs, v_ref[...])


def peer_dense(h_t, u, v, n1, e1, r2, e2, t_tile, e_tile):
    d, ntok = h_t.shape
    n_exp = u.shape[0]
    heads, n_keys, _ = n1.shape
    once = pl.Buffered(1)
    route_spec = pl.BlockSpec((heads, n_keys, t_tile), lambda t, e: (0, 0, t), pipeline_mode=once)
    return pl.pallas_call(
        functools.partial(_peer_dense_kernel, heads=heads, n_keys=n_keys),
        grid=(ntok // t_tile, n_exp // e_tile),
        in_specs=[pl.BlockSpec((d, t_tile), lambda t, e: (0, t), pipeline_mode=once),
                  pl.BlockSpec((e_tile, d), lambda t, e: (e, 0)),
                  pl.BlockSpec((e_tile, d), lambda t, e: (e, 0)),
                  route_spec, route_spec, route_spec, route_spec],
        out_specs=pl.BlockSpec((t_tile, d), lambda t, e: (t, 0)),
        out_shape=jax.ShapeDtypeStruct((ntok, d), f32),
        scratch_shapes=[pltpu.VMEM((e_tile // n_keys, t_tile, n_keys), bf16),
                        pltpu.VMEM((heads, n_keys, t_tile), bf16),
                        pltpu.VMEM((heads, n_keys, t_tile), bf16)],
        compiler_params=_params("parallel", "arbitrary"),
        name="peer_dense",
    )(h_t, u, v, n1, e1, r2, e2)


def _residual_kernel(x_ref, p_ref, g_ref, o_ref):
    o_ref[...] = x_ref[...] + g_ref[0] * p_ref[...]


def _residual_norm_kernel(x_ref, p_ref, g_ref, w_ref, o_ref):
    x = x_ref[...] + g_ref[0] * p_ref[...]
    y = x * lax.rsqrt(jnp.mean(x * x, axis=-1, keepdims=True) + NORM_EPS)
    o_ref[...] = y * w_ref[...]


def gated_residual(x, p, mod3, row_of_tile, k_gate, tm, norm_w=None):
    ntok, d = x.shape
    row_spec = pl.BlockSpec((tm, d), lambda i: (i, 0))
    in_specs = [row_spec, row_spec, pl.BlockSpec((1, 1, d), lambda i: (row_of_tile(i, tm) * 6 + k_gate, 0, 0))]
    args = [x, p, mod3]
    kern = _residual_kernel
    if norm_w is not None:
        in_specs.append(pl.BlockSpec((1, d), lambda i: (0, 0)))
        args.append(norm_w.reshape(1, d))
        kern = _residual_norm_kernel
    return pl.pallas_call(
        kern,
        grid=(ntok // tm,),
        in_specs=in_specs,
        out_specs=row_spec,
        out_shape=jax.ShapeDtypeStruct((ntok, d), f32),
        compiler_params=_params("parallel"),
        name="gated_residual",
    )(*args)


def _axial_rope_tables(n_tokens, hd):
    rows = n_tokens // GRID_W
    row = jnp.repeat(jnp.arange(rows), GRID_W).astype(f32)
    col = jnp.tile(jnp.arange(GRID_W), rows).astype(f32)
    half = hd // 2
    inv = 1.0 / (ROPE_BASE ** (jnp.arange(0, half, 2, dtype=f32) / half))
    ar = row[:, None] * inv[None, :]
    ac = col[:, None] * inv[None, :]
    ang = jnp.concatenate([ar, ar, ac, ac], axis=-1)
    return jnp.cos(ang), jnp.sin(ang)


def kernel(x_prompt, x_sample, cache_attn_k, cache_attn_v, state_ret_fwd, state_ret_bwd, c, c_ctx, w_ada, b_ada,
           norm_mix, norm_ffn, w_in, attn_sink, ret_decay_logit, w_proj_attn, w_proj_ret, w_out, peer_w_query,
           peer_sub_keys, peer_u, peer_v, norm_final):
    batch, seq, d = x_prompt.shape
    dbatch, dseq, _ = x_sample.shape
    depth = w_in.shape[0]
    assert depth >= 1
    n_heads = attn_sink.shape[1]
    _, _, past, n_kv, hd = cache_attn_k.shape
    ret_heads, ret_dk, ret_dv = state_ret_fwd.shape[2:]
    assert ret_dk == ret_dv
    peer_heads, peer_dkey = peer_w_query.shape[2:]
    q_w = n_heads * hd
    kv_w = n_kv * hd
    r_w = ret_heads * ret_dk
    ret_col0 = q_w + 2 * kv_w
    gate_col0 = ret_col0 + 5 * r_w

    n_prompt = batch * seq
    n_lat = dbatch * dseq

    def cond_row_ctx(i, tm):
        return 0

    def cond_row_lat(i, tm):
        return 1 + i // (dseq // tm)

    n_cond = -(-(dbatch + 1) // 8) * 8
    cond = jnp.zeros((n_cond, d), f32).at[0].set(c_ctx).at[1:dbatch + 1].set(c)
    cos, sin = _axial_rope_tables(dseq, hd)
    tm_small = _tile(512, n_prompt, dseq)
    tm_big = _tile(1024, n_prompt, dseq)
    tm_res = _tile(256, n_prompt, dseq)
    tn_in = _tile(1024, w_in.shape[2])
    tn_gate = _tile(512, d, gate_col0)
    tn_q = _tile(1024, peer_heads * peer_dkey)
    e_tile = _tile(512, peer_u.shape[1])

    xp = x_prompt.reshape(n_prompt, d)
    xs = x_sample.reshape(n_lat, d)
    new_k, new_v, new_sf, new_sb = [], [], [], []
    for l in range(depth):
        last = l == depth - 1
        mod3 = ada_modulation(cond, w_ada[l], b_ada[l]).reshape(n_cond * 6, 1, d)
        sink = attn_sink[l].astype(f32)
        log_gamma = jax.nn.log_sigmoid(ret_decay_logit[l].astype(f32))
        w_in_b = w_in[l].astype(bf16)
        w_pa_b = w_proj_attn[l].astype(bf16)
        w_pr_b = w_proj_ret[l].astype(bf16)
        w_out_b = w_out[l].astype(bf16)
        wq_b = peer_w_query[l].reshape(d, peer_heads * peer_dkey).astype(bf16)
        u_b = peer_u[l].astype(bf16)
        v_b = peer_v[l].astype(bf16)

        def channel_mix(x, proj, o_att, o_ret, cond_row):
            merged = merge_branches(o_att, o_ret, w_pa_b, w_pr_b, proj, gate_col0, tm_big, tn_gate)
            x = out_projection(merged, w_out_b, x, mod3, cond_row, 2, tm_big, tn_gate)
            qp, h2 = norm_mod_matmul(x, norm_ffn[l], mod3, cond_row, 3, 4, wq_b, tm_small, tn_q, True)
            n1, e1, r2, e2 = peer_route(qp, peer_sub_keys[l], _tile(1024, x.shape[0]))
            p = peer_dense(h2.T, u_b, v_b, n1, e1, r2, e2, _tile(512, x.shape[0]), e_tile)
            return gated_residual(x, p, mod3, cond_row, 5, tm_res, norm_final if last else None)

        proj = norm_mod_matmul(xp, norm_mix[l], mod3, cond_row_ctx, 0, 1, w_in_b, tm_small, tn_in, False)
        o_att = context_attention(proj, sink, batch, seq, n_heads, n_kv, hd)
        zero_state = jnp.zeros((batch, ret_heads, ret_dk, ret_dv), f32)
        o_ret, s_f, s_b = retention(proj, log_gamma, zero_state, zero_state, batch, seq, ret_col0, ret_heads, ret_dk)
        new_k.append(proj[:, q_w:q_w + kv_w].reshape(batch, seq, n_kv, hd))
        new_v.append(proj[:, q_w + kv_w:q_w + 2 * kv_w].reshape(batch, seq, n_kv, hd))
        new_sf.append(s_f)
        new_sb.append(s_b)
        xp = channel_mix(xp, proj, o_att, o_ret, cond_row_ctx)

        proj = norm_mod_matmul(xs, norm_mix[l], mod3, cond_row_lat, 0, 1, w_in_b, tm_small, tn_in, False)
        o_att = latent_attention(proj, sink, cache_attn_k[:, l].reshape(dbatch, past, kv_w),
                                 cache_attn_v[:, l].reshape(dbatch, past, kv_w), cos, sin,
                                 dbatch, dseq, n_heads, n_kv, hd)
        o_ret, _, _ = retention(proj, log_gamma, state_ret_fwd[:, l], state_ret_bwd[:, l], dbatch, dseq,
                                ret_col0, ret_heads, ret_dk)
        xs = channel_mix(xs, proj, o_att, o_ret, cond_row_lat)

    return (xp.reshape(batch, seq, d), xs.reshape(dbatch, dseq, d), jnp.stack(new_k, axis=1),
            jnp.stack(new_v, axis=1), jnp.stack(new_sf, axis=1), jnp.stack(new_sb, axis=1))
```

```python
import functools

import jax
import jax.numpy as jnp
from jax import lax
from jax.experimental import pallas as pl
from jax.experimental.pallas import tpu as pltpu

GRID_W = 64
WINDOW = 128
ATT_BLOCK = 128
ROPE_BASE = 10000.0
RET_CHUNK = 128
PEER_TOPK = 16
NORM_EPS = 1e-6

LANES = 128
SUBLANES = 8
VMEM_LIMIT_BYTES = 56 * 1024 * 1024
NEG_BIG = -1e30

TM_PROJ, TN_PROJ = 512, 1024
TM_MERGE, TN_MERGE = 1024, 512
TM_RESIDUAL = 256
T_ROUTE = 1024
T_DENSE, E_DENSE = 512, 512

f32 = jnp.float32
bf16 = jnp.bfloat16


def _params(*sem):
    return pltpu.CompilerParams(dimension_semantics=sem, vmem_limit_bytes=VMEM_LIMIT_BYTES)


def _tile(pref, *dims):
    t = pref
    while any(dim % t for dim in dims):
        t //= 2
    assert t >= LANES
    return t


def _dot(a, b):
    return jnp.dot(a, b, preferred_element_type=f32)


def _dot_nt(a, b):
    return lax.dot_general(a, b, (((1,), (1,)), ((), ())), preferred_element_type=f32)


def _ada_kernel(cond_ref, w_ref, b_ref, o_ref):
    c = cond_ref[...]
    s = (c * jax.nn.sigmoid(c)).astype(bf16)
    o_ref[...] = _dot(s, w_ref[...].astype(bf16)) + b_ref[...]


def ada_modulation(cond, w_ada, b_ada):
    r, d = cond.shape
    n = w_ada.shape[1]
    tn = _tile(TN_MERGE, n)
    return pl.pallas_call(
        _ada_kernel,
        grid=(n // tn,),
        in_specs=[pl.BlockSpec((r, d), lambda j: (0, 0)),
                  pl.BlockSpec((d, tn), lambda j: (0, j)),
                  pl.BlockSpec((1, tn), lambda j: (0, j))],
        out_specs=pl.BlockSpec((r, tn), lambda j: (0, j)),
        out_shape=jax.ShapeDtypeStruct((r, n), f32),
        compiler_params=_params("parallel"),
        name="ada_modulation",
    )(cond, w_ada, b_ada.reshape(1, n))


def _norm_mod(x, g, sh, sc):
    y = x * lax.rsqrt(jnp.mean(x * x, axis=-1, keepdims=True) + NORM_EPS)
    y = y * g
    return y * (1.0 + sc) + sh


def _norm_mod_matmul_kernel(x_ref, g_ref, sh_ref, sc_ref, w_ref, o_ref, h_scr):
    @pl.when(pl.program_id(1) == 0)
    def _():
        h_scr[...] = _norm_mod(x_ref[...], g_ref[...], sh_ref[0], sc_ref[0]).astype(bf16)

    o_ref[...] = _dot(h_scr[...], w_ref[...])


def _norm_mod_matmul_h_kernel(x_ref, g_ref, sh_ref, sc_ref, w_ref, o_ref, h_ref):
    @pl.when(pl.program_id(1) == 0)
    def _():
        h_ref[...] = _norm_mod(x_ref[...], g_ref[...], sh_ref[0], sc_ref[0]).astype(bf16)

    o_ref[...] = _dot(h_ref[...], w_ref[...])


def norm_mod_matmul(x, g, mod3, row_of_tile, k_shift, k_scale, w, tm, tn, emit_h):
    ntok, d = x.shape
    n = w.shape[1]

    def mod_spec(k):
        return pl.BlockSpec((1, 1, d), lambda i, j: (row_of_tile(i, tm) * 6 + k, 0, 0))

    in_specs = [pl.BlockSpec((tm, d), lambda i, j: (i, 0)),
                pl.BlockSpec((1, d), lambda i, j: (0, 0)),
                mod_spec(k_shift), mod_spec(k_scale),
                pl.BlockSpec((d, tn), lambda i, j: (0, j))]
    o_spec = pl.BlockSpec((tm, tn), lambda i, j: (i, j))
    if emit_h:
        return pl.pallas_call(
            _norm_mod_matmul_h_kernel,
            grid=(ntok // tm, n // tn),
            in_specs=in_specs,
            out_specs=[o_spec, pl.BlockSpec((tm, d), lambda i, j: (i, 0))],
            out_shape=[jax.ShapeDtypeStruct((ntok, n), f32), jax.ShapeDtypeStruct((ntok, d), bf16)],
            compiler_params=_params("parallel", "arbitrary"),
            name="norm_mod_matmul_h",
        )(x, g.reshape(1, d), mod3, mod3, w)
    return pl.pallas_call(
        _norm_mod_matmul_kernel,
        grid=(ntok // tm, n // tn),
        in_specs=in_specs,
        out_specs=o_spec,
        out_shape=jax.ShapeDtypeStruct((ntok, n), f32),
        scratch_shapes=[pltpu.VMEM((tm, d), bf16)],
        compiler_params=_params("parallel", "arbitrary"),
        name="norm_mod_matmul",
    )(x, g.reshape(1, d), mod3, mod3, w)


def _softmax_pv(logits, values, sink):
    m = sink
    for lg in logits:
        m = jnp.maximum(m, jnp.max(lg, axis=-1, keepdims=True))
    den = jnp.exp(sink - m)
    acc = None
    for lg, v in zip(logits, values):
        p = jnp.exp(lg - m)
        den = den + jnp.sum(p, axis=-1, keepdims=True)
        pv = _dot(p.astype(bf16), v)
        acc = pv if acc is None else acc + pv
    return acc / den


def _sink_column(sink_ref, kvh, group, rows):
    return jnp.concatenate([jnp.full((rows, 1), sink_ref[kvh * group + g], f32) for g in range(group)], axis=0)


def _ctx_attn_kernel(sink_ref, q_ref, k_ref, v_ref, o_ref, *, group, hd, n_kv):
    scale = hd ** -0.5
    for kvh in range(n_kv):
        k = k_ref[:, kvh * hd:(kvh + 1) * hd].astype(bf16)
        v = v_ref[:, kvh * hd:(kvh + 1) * hd].astype(bf16)
        for g in range(group):
            cols = slice((kvh * group + g) * hd, (kvh * group + g + 1) * hd)
            lg = _dot_nt(q_ref[:, cols].astype(bf16), k) * scale
            o = _softmax_pv([lg], [v], sink_ref[kvh * group + g])
            o_ref[:, cols] = o.astype(o_ref.dtype)


def context_attention(proj, sink, nb, seq, n_heads, n_kv, hd):
    group = n_heads // n_kv
    q_w = n_heads * hd
    kv_w = n_kv * hd
    assert q_w % kv_w == 0
    kcol = q_w // kv_w
    return pl.pallas_call(
        functools.partial(_ctx_attn_kernel, group=group, hd=hd, n_kv=n_kv),
        grid=(nb,),
        in_specs=[pl.BlockSpec(memory_space=pltpu.SMEM),
                  pl.BlockSpec((seq, q_w), lambda b: (b, 0)),
                  pl.BlockSpec((seq, kv_w), lambda b: (b, kcol)),
                  pl.BlockSpec((seq, kv_w), lambda b: (b, kcol + 1))],
        out_specs=pl.BlockSpec((seq, q_w), lambda b: (b, 0)),
        out_shape=jax.ShapeDtypeStruct((nb * seq, q_w), bf16),
        compiler_params=_params("parallel"),
        name="context_attention",
    )(sink, proj, proj, proj)


def _rope(x, cos, sin, first_half):
    rot = jnp.where(first_half, -pltpu.roll(x, 3 * (LANES // 4), 1), pltpu.roll(x, LANES // 4, 1))
    return x * cos + rot * sin


def _lat_attn_kernel(sink_ref, q_ref, kp_ref, kc_ref, kn_ref, vp_ref, vc_ref, vn_ref, kx_ref, vx_ref,
                     cp_ref, cc_ref, cn_ref, sp_ref, sc_ref, sn_ref, o_ref, *, group, hd, n_kv):
    n = pl.program_id(1)
    nblk = pl.num_programs(1)
    scale = hd ** -0.5
    blk = (ATT_BLOCK, hd)
    lane = lax.broadcasted_iota(jnp.int32, blk, 1)
    first_half = (lane % (hd // 2)) < (hd // 4)
    row = lax.broadcasted_iota(jnp.int32, (group * ATT_BLOCK, ATT_BLOCK), 0) % ATT_BLOCK
    col = lax.broadcasted_iota(jnp.int32, (group * ATT_BLOCK, ATT_BLOCK), 1)
    mask_p = jnp.logical_and(col >= row, n > 0)
    mask_n = jnp.logical_and(col <= row, n < nblk - 1)
    for kvh in range(n_kv):
        ks = slice(kvh * hd, (kvh + 1) * hd)
        kp = _rope(kp_ref[:, ks], cp_ref[...], sp_ref[...], first_half).astype(bf16)
        kc = _rope(kc_ref[:, ks], cc_ref[...], sc_ref[...], first_half).astype(bf16)
        kn = _rope(kn_ref[:, ks], cn_ref[...], sn_ref[...], first_half).astype(bf16)
        kx = kx_ref[0, :, ks].astype(bf16)
        values = [vp_ref[:, ks].astype(bf16), vc_ref[:, ks].astype(bf16), vn_ref[:, ks].astype(bf16),
                  vx_ref[0, :, ks].astype(bf16)]
        q0 = kvh * group * hd
        q = jnp.concatenate([_rope(q_ref[:, q0 + g * hd:q0 + (g + 1) * hd], cc_ref[...], sc_ref[...],
                                   first_half).astype(bf16) for g in range(group)], axis=0)
        lp = jnp.where(mask_p, _dot_nt(q, kp) * scale, NEG_BIG)
        lc = _dot_nt(q, kc) * scale
        ln = jnp.where(mask_n, _dot_nt(q, kn) * scale, NEG_BIG)
        lx = _dot_nt(q, kx) * scale
        o = _softmax_pv([lp, lc, ln, lx], values, _sink_column(sink_ref, kvh, group, ATT_BLOCK))
        for g in range(group):
            o_ref[:, q0 + g * hd:q0 + (g + 1) * hd] = o[g * ATT_BLOCK:(g + 1) * ATT_BLOCK].astype(o_ref.dtype)


def latent_attention(proj, sink, k_ctx, v_ctx, cos, sin, nb, seq, n_heads, n_kv, hd):
    assert WINDOW == ATT_BLOCK
    group = n_heads // n_kv
    q_w = n_heads * hd
    kv_w = n_kv * hd
    assert q_w % kv_w == 0
    kcol = q_w // kv_w
    vcol = kcol + 1
    nblk = seq // ATT_BLOCK
    past = k_ctx.shape[1]

    def rb(b, n):
        return b * nblk + n

    def prev(n):
        return jnp.maximum(n - 1, 0)

    def nxt(n):
        return jnp.minimum(n + 1, nblk - 1)

    def kv_spec(col, f):
        return pl.BlockSpec((ATT_BLOCK, kv_w), lambda b, n: (rb(b, f(n)), col))

    def tab_spec(f):
        return pl.BlockSpec((ATT_BLOCK, hd), lambda b, n: (f(n), 0))

    ident = lambda n: n
    ctx_spec = pl.BlockSpec((1, past, kv_w), lambda b, n: (b, 0, 0))
    return pl.pallas_call(
        functools.partial(_lat_attn_kernel, group=group, hd=hd, n_kv=n_kv),
        grid=(nb, nblk),
        in_specs=[pl.BlockSpec(memory_space=pltpu.SMEM),
                  pl.BlockSpec((ATT_BLOCK, q_w), lambda b, n: (rb(b, n), 0)),
                  kv_spec(kcol, prev), kv_spec(kcol, ident), kv_spec(kcol, nxt),
                  kv_spec(vcol, prev), kv_spec(vcol, ident), kv_spec(vcol, nxt),
                  ctx_spec, ctx_spec,
                  tab_spec(prev), tab_spec(ident), tab_spec(nxt),
                  tab_spec(prev), tab_spec(ident), tab_spec(nxt)],
        out_specs=pl.BlockSpec((ATT_BLOCK, q_w), lambda b, n: (rb(b, n), 0)),
        out_shape=jax.ShapeDtypeStruct((nb * seq, q_w), bf16),
        compiler_params=_params("parallel", "parallel"),
        name="latent_attention",
    )(sink, proj, proj, proj, proj, proj, proj, proj, k_ctx, v_ctx, cos, cos, cos, sin, sin, sin)


RET_HEADS_PER_STEP_CHOICES = (4, 2)


def _ret_kernel(lg_ref, q_ref, k_ref, v_ref, gf_ref, gb_ref, s0f_ref, s0b_ref, o_ref, sf_ref, sb_ref,
                of_scr, ob_scr, *, n_chunks, dk, hps):
    hb = pl.program_id(1)
    c = RET_CHUNK
    ii = lax.broadcasted_iota(jnp.int32, (c, c), 0).astype(f32)
    jj = lax.broadcasted_iota(jnp.int32, (c, c), 1).astype(f32)
    d = ii - jj
    ri = lax.broadcasted_iota(jnp.int32, (c, dk), 0).astype(f32)
    kscale = dk ** -0.5

    consts = []
    for hh in range(hps):
        lgf = lg_ref[0, hb * hps + hh]
        lgb = lg_ref[1, hb * hps + hh]
        fwd = (jnp.where(d >= 0, jnp.exp(jnp.maximum(d, 0.0) * lgf), 0.0),
               jnp.exp((ri + 1.0) * lgf), jnp.exp((c - 1.0 - ri) * lgf), jnp.exp(jnp.full((dk, dk), c * lgf, f32)))
        bwd = (jnp.where(d <= 0, jnp.exp(jnp.maximum(-d, 0.0) * lgb), 0.0),
               jnp.exp((c - ri) * lgb), jnp.exp(ri * lgb), jnp.exp(jnp.full((dk, dk), c * lgb, f32)))
        consts.append((fwd, bwd))

    def body(n, states):
        chains = []
        for hh in range(hps):
            for direction, (chunk_idx, o_scr) in enumerate(((n, of_scr), (n_chunks - 1 - n, ob_scr))):
                decay, qdec, kdec, cd = consts[hh][direction]
                r = pl.ds(pl.multiple_of(chunk_idx * c, c), c)
                cs = slice(hh * dk, (hh + 1) * dk)
                q = q_ref[r, cs]
                k = k_ref[r, cs] * kscale
                vb = v_ref[r, cs].astype(bf16)
                state = states[hh][direction]
                raw = _dot_nt(q.astype(bf16), k.astype(bf16))
                cross = _dot((q * qdec).astype(bf16), state.astype(bf16))
                kv = _dot((k * kdec).T.astype(bf16), vb)
                chains.append((raw, cross, kv, vb, decay, cd, state, o_scr, r, cs))
        scores = [(raw * decay).astype(bf16) for raw, _, _, _, decay, *_ in chains]
        new = []
        for (raw, cross, kv, vb, decay, cd, state, o_scr, r, cs), sc in zip(chains, scores):
            o_scr[r, cs] = _dot(sc, vb) + cross
            new.append(cd * state + kv)
        return tuple((new[2 * hh], new[2 * hh + 1]) for hh in range(hps))

    init = tuple((s0f_ref[0, hh], s0b_ref[0, hh]) for hh in range(hps))
    final = lax.fori_loop(0, n_chunks, body, init)
    for hh in range(hps):
        sf_ref[0, hh] = final[hh][0]
        sb_ref[0, hh] = final[hh][1]

    def hn(o):
        return o * lax.rsqrt(jnp.mean(o * o, axis=-1, keepdims=True) + NORM_EPS)

    def finish(n, carry):
        r = pl.ds(pl.multiple_of(n * c, c), c)
        for hh in range(hps):
            cs = slice(hh * dk, (hh + 1) * dk)
            gf = gf_ref[r, cs]
            gb = gb_ref[r, cs]
            out = hn(of_scr[r, cs]) * (gf * jax.nn.sigmoid(gf)) + hn(ob_scr[r, cs]) * (gb * jax.nn.sigmoid(gb))
            o_ref[r, cs] = out.astype(o_ref.dtype)
        return carry

    lax.fori_loop(0, n_chunks, finish, 0)


def retention(proj, log_gamma, s0f, s0b, nb, seq, col0, n_heads, dk):
    n_chunks = seq // RET_CHUNK
    hps = 1
    for cand in RET_HEADS_PER_STEP_CHOICES:
        if n_heads % cand == 0 and seq * cand * dk * 4 * 12 <= VMEM_LIMIT_BYTES // 2:
            hps = cand
            break
    w = hps * dk
    c0 = col0 // w
    gpb = n_heads // hps

    def col_spec(k):
        return pl.BlockSpec((seq, w), lambda b, h: (b, c0 + k * gpb + h))

    st_spec = pl.BlockSpec((1, hps, dk, dk), lambda b, h: (b, h, 0, 0))
    st_shape = jax.ShapeDtypeStruct((nb, n_heads, dk, dk), f32)
    return pl.pallas_call(
        functools.partial(_ret_kernel, n_chunks=n_chunks, dk=dk, hps=hps),
        grid=(nb, gpb),
        in_specs=[pl.BlockSpec(memory_space=pltpu.SMEM),
                  col_spec(0), col_spec(1), col_spec(2), col_spec(3), col_spec(4),
                  st_spec, st_spec],
        out_specs=[pl.BlockSpec((seq, w), lambda b, h: (b, h)), st_spec, st_spec],
        out_shape=[jax.ShapeDtypeStruct((nb * seq, n_heads * dk), bf16), st_shape, st_shape],
        scratch_shapes=[pltpu.VMEM((seq, w), f32), pltpu.VMEM((seq, w), f32)],
        compiler_params=_params("parallel", "parallel"),
        name="retention",
    )(log_gamma, proj, proj, proj, proj, proj, s0f, s0b)


def _merge_kernel(oa_ref, or_ref, wa_ref, wr_ref, ga_ref, gr_ref, o_ref):
    a = _dot(oa_ref[...], wa_ref[...])
    r = _dot(or_ref[...], wr_ref[...])
    o_ref[...] = (jax.nn.sigmoid(ga_ref[...]) * a + jax.nn.sigmoid(gr_ref[...]) * r).astype(o_ref.dtype)


def merge_branches(o_att, o_ret, w_pa, w_pr, proj, gate_col0, tm, tn):
    ntok, qw = o_att.shape
    rw = o_ret.shape[1]
    d = w_pa.shape[1]
    ga0 = gate_col0 // tn
    gr0 = (gate_col0 + d) // tn
    return pl.pallas_call(
        _merge_kernel,
        grid=(ntok // tm, d // tn),
        in_specs=[pl.BlockSpec((tm, qw), lambda i, j: (i, 0)),
                  pl.BlockSpec((tm, rw), lambda i, j: (i, 0)),
                  pl.BlockSpec((qw, tn), lambda i, j: (0, j)),
                  pl.BlockSpec((rw, tn), lambda i, j: (0, j)),
                  pl.BlockSpec((tm, tn), lambda i, j: (i, ga0 + j)),
                  pl.BlockSpec((tm, tn), lambda i, j: (i, gr0 + j))],
        out_specs=pl.BlockSpec((tm, tn), lambda i, j: (i, j)),
        out_shape=jax.ShapeDtypeStruct((ntok, d), bf16),
        compiler_params=_params("parallel", "arbitrary"),
        name="merge_branches",
    )(o_att, o_ret, w_pa, w_pr, proj, proj)


def _out_proj_kernel(m_ref, w_ref, x_ref, g_ref, o_ref):
    o_ref[...] = x_ref[...] + g_ref[0] * _dot(m_ref[...], w_ref[...])


def out_projection(merged, w_out, x, mod3, row_of_tile, k_gate, tm, tn):
    ntok, d = x.shape
    kd = merged.shape[1]
    return pl.pallas_call(
        _out_proj_kernel,
        grid=(ntok // tm, d // tn),
        in_specs=[pl.BlockSpec((tm, kd), lambda i, j: (i, 0)),
                  pl.BlockSpec((kd, tn), lambda i, j: (0, j)),
                  pl.BlockSpec((tm, tn), lambda i, j: (i, j)),
                  pl.BlockSpec((1, 1, tn), lambda i, j: (row_of_tile(i, tm) * 6 + k_gate, 0, j))],
        out_specs=pl.BlockSpec((tm, tn), lambda i, j: (i, j)),
        out_shape=jax.ShapeDtypeStruct((ntok, d), f32),
        compiler_params=_params("parallel", "arbitrary"),
        name="out_projection",
    )(merged, w_out, x, mod3)


def _extract_round(r, x_ref, rank_ref, vals_ref, pos, sentinel):
    x = x_ref[...]
    m = jnp.max(x, axis=0, keepdims=True)
    idx = jnp.min(jnp.where(x == m, pos, sentinel), axis=0, keepdims=True)
    sel = pos == idx
    rank_ref[...] = jnp.where(sel, lax.convert_element_type(r, f32), rank_ref[...])
    x_ref[...] = jnp.where(sel, -jnp.inf, x)
    vals_ref[pl.ds(r, 1), :] = m


def _candidate_layout(k):
    slabs = []
    for a in range(k // 2):
        n_valid = k // (a + 1)
        slabs.append((a, n_valid, -(-n_valid // SUBLANES) * SUBLANES))
    return slabs


def _peer_route_kernel(q_ref, keys_ref, n1_ref, e1_ref, r2_ref, e2_ref, x1_scr, x2_scr, v1_scr, v2_scr,
                       c_scr, crank_scr, cv_scr, rank2_scr, e2_scr, *, half):
    k = PEER_TOPK
    n_keys, t = x1_scr.shape
    s1 = _dot_nt(keys_ref[0, 0].astype(bf16), q_ref[:, :half].astype(bf16))
    s2 = _dot_nt(keys_ref[0, 1].astype(bf16), q_ref[:, half:].astype(bf16))
    x1_scr[...] = s1
    x2_scr[...] = s2
    e1_ref[0] = jnp.exp(s1 - jnp.max(s1, axis=0, keepdims=True))
    e2_scr[...] = jnp.exp(s2 - jnp.max(s2, axis=0, keepdims=True))
    n1_ref[0] = jnp.full((n_keys, t), float(k), f32)
    rank2_scr[...] = jnp.full((n_keys, t), float(k), f32)
    key_pos = lax.broadcasted_iota(jnp.int32, (n_keys, t), 0).astype(f32)

    def sub_round(r, carry):
        _extract_round(r, x1_scr, n1_ref.at[0], v1_scr, key_pos, float(n_keys))
        _extract_round(r, x2_scr, rank2_scr, v2_scr, key_pos, float(n_keys))
        return carry

    lax.fori_loop(0, k, sub_round, 0)

    v1 = v1_scr[...]
    v2 = v2_scr[...]
    slabs = _candidate_layout(k)
    cands, poss = [], []
    for a, n_valid, n_rows in slabs:
        b = lax.broadcasted_iota(jnp.int32, (n_rows, t), 0)
        cands.append(jnp.where(b < n_valid, v1[a:a + 1, :] + v2[:n_rows, :], -jnp.inf))
        poss.append((a * k + b).astype(f32))
    a_hi = lax.broadcasted_iota(jnp.int32, (k // 2, t), 0) + k // 2
    cands.append(v1[k // 2:, :] + v2[0:1, :])
    poss.append((a_hi * k).astype(f32))
    c_scr[...] = jnp.concatenate(cands, axis=0)
    cand_pos = jnp.concatenate(poss, axis=0)
    crank_scr[...] = jnp.full(c_scr.shape, float(k), f32)

    def cand_round(r, carry):
        _extract_round(r, c_scr, crank_scr, cv_scr, cand_pos, float(k * k))
        return carry

    lax.fori_loop(0, k, cand_round, 0)

    picked = jnp.where(crank_scr[...] < float(k), 1.0, 0.0)
    cnt = []
    row0 = 0
    for a, n_valid, n_rows in slabs:
        cnt.append(jnp.sum(picked[row0:row0 + n_rows, :], axis=0, keepdims=True))
        row0 += n_rows
    for a in range(k // 2, k):
        cnt.append(picked[row0 + a - k // 2:row0 + a - k // 2 + 1, :])
    cvals = cv_scr[...]
    z = jnp.sum(jnp.exp(cvals - cvals[0:1, :]), axis=0, keepdims=True)
    rank1 = n1_ref[0]
    n1 = jnp.zeros_like(rank1)
    for a in range(k):
        n1 = jnp.where(rank1 == float(a), cnt[a], n1)
    n1_ref[0] = n1
    r2_ref[0] = rank2_scr[...].astype(bf16)
    e2_ref[0] = (e2_scr[...] / z).astype(bf16)


def peer_route(qp, sub_keys, t_tile):
    ntok = qp.shape[0]
    heads, _, n_keys, half = sub_keys.shape
    k = PEER_TOPK
    n_cand = sum(s[2] for s in _candidate_layout(k)) + k // 2
    out_spec = pl.BlockSpec((1, n_keys, t_tile), lambda i, h: (h, 0, i))
    out_shape = jax.ShapeDtypeStruct((heads, n_keys, ntok), f32)
    out_shape_b = jax.ShapeDtypeStruct((heads, n_keys, ntok), bf16)
    return pl.pallas_call(
        functools.partial(_peer_route_kernel, half=half),
        grid=(ntok // t_tile, heads),
        in_specs=[pl.BlockSpec((t_tile, 2 * half), lambda i, h: (i, h)),
                  pl.BlockSpec((1, 2, n_keys, half), lambda i, h: (h, 0, 0, 0))],
        out_specs=[out_spec] * 4,
        out_shape=[out_shape, out_shape, out_shape_b, out_shape_b],
        scratch_shapes=[pltpu.VMEM((n_keys, t_tile), f32), pltpu.VMEM((n_keys, t_tile), f32),
                        pltpu.VMEM((k, t_tile), f32), pltpu.VMEM((k, t_tile), f32),
                        pltpu.VMEM((n_cand, t_tile), f32), pltpu.VMEM((n_cand, t_tile), f32),
                        pltpu.VMEM((k, t_tile), f32),
                        pltpu.VMEM((n_keys, t_tile), f32), pltpu.VMEM((n_keys, t_tile), f32)],
        compiler_params=_params("parallel", "parallel"),
        name="peer_route",
    )(qp, sub_keys)


BF16_SUBLANES = 2 * SUBLANES


def _bcast_rows_bf16(row, n_rows):
    tile = jnp.broadcast_to(row, (BF16_SUBLANES, row.shape[1])).astype(bf16)
    return jnp.concatenate([tile] * (n_rows // BF16_SUBLANES), axis=0)


def _peer_dense_kernel(ht_ref, u_ref, v_ref, n1_ref, e1_ref, r2_ref, e2_ref, acc_ref, coef_scr, r2_scr, e2_scr,
                       *, heads, n_keys):
    e = pl.program_id(1)
    et = u_ref.shape[0]
    t = ht_ref.shape[1]
    cpt = et // n_keys

    @pl.when(e == 0)
    def _():
        acc_ref[...] = jnp.zeros_like(acc_ref)
        r2_scr[...] = r2_ref[...]
        e2_scr[...] = e2_ref[...]

    act = _dot(u_ref[...], ht_ref[...])
    for c in range(cpt):
        i = e * cpt + c
        n1_rows = [n1_ref[h, pl.ds(i, 1), :] for h in range(heads)]
        e1_rows = [e1_ref[h, pl.ds(i, 1), :] for h in range(heads)]
        for tk in range(t // LANES):
            cols = slice(tk * LANES, (tk + 1) * LANES)
            gate = None
            for h in range(heads):
                n1 = _bcast_rows_bf16(n1_rows[h][:, cols], n_keys)
                e1 = _bcast_rows_bf16(e1_rows[h][:, cols], n_keys)
                term = jnp.where(r2_scr[h, :, cols] < n1, e2_scr[h, :, cols] * e1, jnp.zeros((), bf16))
                gate = term if gate is None else gate + term
            a = act[c * n_keys:(c + 1) * n_keys, cols].astype(bf16)
            gelu = 0.5 * a * (1.0 + lax.erf(a * (2.0 ** -0.5)))
            coef_scr[c, cols, :] = (gate * gelu).T
    lhs = jnp.concatenate([coef_scr[c] for c in range(cpt)], axis=-1)
    acc_ref[...] += _dot(lhs, v_ref[...])


def peer_dense(h_t, u, v, n1, e1, r2, e2, t_tile, e_tile):
    d, ntok = h_t.shape
    n_exp = u.shape[0]
    heads, n_keys, _ = n1.shape
    once = pl.Buffered(1)
    route_spec = pl.BlockSpec((heads, n_keys, t_tile), lambda t, e: (0, 0, t), pipeline_mode=once)
    return pl.pallas_call(
        functools.partial(_peer_dense_kernel, heads=heads, n_keys=n_keys),
        grid=(ntok // t_tile, n_exp // e_tile),
        in_specs=[pl.BlockSpec((d, t_tile), lambda t, e: (0, t), pipeline_mode=once),
                  pl.BlockSpec((e_tile, d), lambda t, e: (e, 0)),
                  pl.BlockSpec((e_tile, d), lambda t, e: (e, 0)),
                  route_spec, route_spec, route_spec, route_spec],
        out_specs=pl.BlockSpec((t_tile, d), lambda t, e: (t, 0)),
        out_shape=jax.ShapeDtypeStruct((ntok, d), f32),
        scratch_shapes=[pltpu.VMEM((e_tile // n_keys, t_tile, n_keys), bf16),
                        pltpu.VMEM((heads, n_keys, t_tile), bf16),
                        pltpu.VMEM((heads, n_keys, t_tile), bf16)],
        compiler_params=_params("parallel", "arbitrary"),
        name="peer_dense",
    )(h_t, u, v, n1, e1, r2, e2)


def _residual_kernel(x_ref, p_ref, g_ref, o_ref):
    o_ref[...] = x_ref[...] + g_ref[0] * p_ref[...]


def _residual_norm_kernel(x_ref, p_ref, g_ref, w_ref, o_ref):
    x = x_ref[...] + g_ref[0] * p_ref[...]
    y = x * lax.rsqrt(jnp.mean(x * x, axis=-1, keepdims=True) + NORM_EPS)
    o_ref[...] = y * w_ref[...]


def gated_residual(x, p, mod3, row_of_tile, k_gate, tm, norm_w=None):
    ntok, d = x.shape
    row_spec = pl.BlockSpec((tm, d), lambda i: (i, 0))
    in_specs = [row_spec, row_spec, pl.BlockSpec((1, 1, d), lambda i: (row_of_tile(i, tm) * 6 + k_gate, 0, 0))]
    args = [x, p, mod3]
    kern = _residual_kernel
    if norm_w is not None:
        in_specs.append(pl.BlockSpec((1, d), lambda i: (0, 0)))
        args.append(norm_w.reshape(1, d))
        kern = _residual_norm_kernel
    return pl.pallas_call(
        kern,
        grid=(ntok // tm,),
        in_specs=in_specs,
        out_specs=row_spec,
        out_shape=jax.ShapeDtypeStruct((ntok, d), f32),
        compiler_params=_params("parallel"),
        name="gated_residual",
    )(*args)


def _axial_rope_tables(n_tokens, hd):
    rows = n_tokens // GRID_W
    row = jnp.repeat(jnp.arange(rows), GRID_W).astype(f32)
    col = jnp.tile(jnp.arange(GRID_W), rows).astype(f32)
    half = hd // 2
    inv = 1.0 / (ROPE_BASE ** (jnp.arange(0, half, 2, dtype=f32) / half))
    ar = row[:, None] * inv[None, :]
    ac = col[:, None] * inv[None, :]
    ang = jnp.concatenate([ar, ar, ac, ac], axis=-1)
    return jnp.cos(ang), jnp.sin(ang)


def kernel(x_prompt, x_sample, cache_attn_k, cache_attn_v, state_ret_fwd, state_ret_bwd, c, c_ctx, w_ada, b_ada,
           norm_mix, norm_ffn, w_in, attn_sink, ret_decay_logit, w_proj_attn, w_proj_ret, w_out, peer_w_query,
           peer_sub_keys, peer_u, peer_v, norm_final):
    batch, seq, d = x_prompt.shape
    dbatch, dseq, _ = x_sample.shape
    depth = w_in.shape[0]
    assert depth >= 1
    n_heads = attn_sink.shape[1]
    _, _, past, n_kv, hd = cache_attn_k.shape
    ret_heads, ret_dk, ret_dv = state_ret_fwd.shape[2:]
    assert ret_dk == ret_dv
    peer_heads, peer_dkey = peer_w_query.shape[2:]
    q_w = n_heads * hd
    kv_w = n_kv * hd
    r_w = ret_heads * ret_dk
    ret_col0 = q_w + 2 * kv_w
    gate_col0 = ret_col0 + 5 * r_w

    n_prompt = batch * seq
    n_lat = dbatch * dseq

    def cond_row_ctx(i, tm):
        return 0

    def cond_row_lat(i, tm):
        return 1 + i // (dseq // tm)

    n_cond = -(-(dbatch + 1) // SUBLANES) * SUBLANES
    cond = jnp.zeros((n_cond, d), f32).at[0].set(c_ctx).at[1:dbatch + 1].set(c)
    cos, sin = _axial_rope_tables(dseq, hd)
    tm_small = _tile(TM_PROJ, n_prompt, dseq)
    tm_big = _tile(TM_MERGE, n_prompt, dseq)
    tm_res = _tile(TM_RESIDUAL, n_prompt, dseq)
    tn_in = _tile(TN_PROJ, w_in.shape[2])
    tn_gate = _tile(TN_MERGE, d, gate_col0)
    tn_q = _tile(TN_PROJ, peer_heads * peer_dkey)
    e_tile = _tile(E_DENSE, peer_u.shape[1])

    xp = x_prompt.reshape(n_prompt, d)
    xs = x_sample.reshape(n_lat, d)
    new_k, new_v, new_sf, new_sb = [], [], [], []
    for l in range(depth):
        last = l == depth - 1
        mod3 = ada_modulation(cond, w_ada[l], b_ada[l]).reshape(n_cond * 6, 1, d)
        sink = attn_sink[l].astype(f32)
        log_gamma = jax.nn.log_sigmoid(ret_decay_logit[l].astype(f32))
        w_in_b = w_in[l].astype(bf16)
        w_pa_b = w_proj_attn[l].astype(bf16)
        w_pr_b = w_proj_ret[l].astype(bf16)
        w_out_b = w_out[l].astype(bf16)
        wq_b = peer_w_query[l].reshape(d, peer_heads * peer_dkey).astype(bf16)
        u_b = peer_u[l].astype(bf16)
        v_b = peer_v[l].astype(bf16)

        def channel_mix(x, proj, o_att, o_ret, cond_row):
            merged = merge_branches(o_att, o_ret, w_pa_b, w_pr_b, proj, gate_col0, tm_big, tn_gate)
            x = out_projection(merged, w_out_b, x, mod3, cond_row, 2, tm_big, tn_gate)
            qp, h2 = norm_mod_matmul(x, norm_ffn[l], mod3, cond_row, 3, 4, wq_b, tm_small, tn_q, True)
            n1, e1, r2, e2 = peer_route(qp, peer_sub_keys[l], _tile(T_ROUTE, x.shape[0]))
            p = peer_dense(h2.T, u_b, v_b, n1, e1, r2, e2, _tile(T_DENSE, x.shape[0]), e_tile)
            return gated_residual(x, p, mod3, cond_row, 5, tm_res, norm_final if last else None)

        proj = norm_mod_matmul(xp, norm_mix[l], mod3, cond_row_ctx, 0, 1, w_in_b, tm_small, tn_in, False)
        o_att = context_attention(proj, sink, batch, seq, n_heads, n_kv, hd)
        zero_state = jnp.zeros((batch, ret_heads, ret_dk, ret_dv), f32)
        o_ret, s_f, s_b = retention(proj, log_gamma, zero_state, zero_state, batch, seq, ret_col0, ret_heads, ret_dk)
        new_k.append(proj[:, q_w:q_w + kv_w].reshape(batch, seq, n_kv, hd))
        new_v.append(proj[:, q_w + kv_w:q_w + 2 * kv_w].reshape(batch, seq, n_kv, hd))
        new_sf.append(s_f)
        new_sb.append(s_b)
        xp = channel_mix(xp, proj, o_att, o_ret, cond_row_ctx)

        proj = norm_mod_matmul(xs, norm_mix[l], mod3, cond_row_lat, 0, 1, w_in_b, tm_small, tn_in, False)
        o_att = latent_attention(proj, sink, cache_attn_k[:, l].reshape(dbatch, past, kv_w),
                                 cache_attn_v[:, l].reshape(dbatch, past, kv_w), cos, sin,
                                 dbatch, dseq, n_heads, n_kv, hd)
        o_ret, _, _ = retention(proj, log_gamma, state_ret_fwd[:, l], state_ret_bwd[:, l], dbatch, dseq,
                                ret_col0, ret_heads, ret_dk)
        xs = channel_mix(xs, proj, o_att, o_ret, cond_row_lat)

    return (xp.reshape(batch, seq, d), xs.reshape(dbatch, dseq, d), jnp.stack(new_k, axis=1),
            jnp.stack(new_v, axis=1), jnp.stack(new_sf, axis=1), jnp.stack(new_sb, axis=1))
```

```python
import functools

import jax
import jax.numpy as jnp
from jax import lax
from jax.experimental import pallas as pl
from jax.experimental.pallas import tpu as pltpu

GRID_W = 64
WINDOW = 128
ATT_BLOCK = 128
ROPE_BASE = 10000.0
RET_CHUNK = 128
PEER_TOPK = 16
NORM_EPS = 1e-6

LANES = 128
SUBLANES = 8
VMEM_LIMIT_BYTES = 56 * 1024 * 1024
NEG_BIG = -1e30

TM_PROJ, TN_PROJ = 512, 1024
TM_MERGE, TN_MERGE = 1024, 512
TM_RESIDUAL = 256
T_ROUTE = 1024
T_DENSE, E_DENSE = 512, 1024

f32 = jnp.float32
bf16 = jnp.bfloat16


def _params(*sem):
    return pltpu.CompilerParams(dimension_semantics=sem, vmem_limit_bytes=VMEM_LIMIT_BYTES)


def _tile(pref, *dims):
    t = pref
    while any(dim % t for dim in dims):
        t //= 2
    assert t >= LANES
    return t


def _dot(a, b):
    return jnp.dot(a, b, preferred_element_type=f32)


def _dot_nt(a, b):
    return lax.dot_general(a, b, (((1,), (1,)), ((), ())), preferred_element_type=f32)


def _ada_kernel(cond_ref, w_ref, b_ref, o_ref):
    c = cond_ref[...]
    s = (c * jax.nn.sigmoid(c)).astype(bf16)
    o_ref[...] = _dot(s, w_ref[...].astype(bf16)) + b_ref[...]


def ada_modulation(cond, w_ada, b_ada):
    r, d = cond.shape
    n = w_ada.shape[1]
    tn = _tile(TN_MERGE, n)
    return pl.pallas_call(
        _ada_kernel,
        grid=(n // tn,),
        in_specs=[pl.BlockSpec((r, d), lambda j: (0, 0)),
                  pl.BlockSpec((d, tn), lambda j: (0, j)),
                  pl.BlockSpec((1, tn), lambda j: (0, j))],
        out_specs=pl.BlockSpec((r, tn), lambda j: (0, j)),
        out_shape=jax.ShapeDtypeStruct((r, n), f32),
        compiler_params=_params("parallel"),
        name="ada_modulation",
    )(cond, w_ada, b_ada.reshape(1, n))


def _norm_mod(x, g, sh, sc):
    y = x * lax.rsqrt(jnp.mean(x * x, axis=-1, keepdims=True) + NORM_EPS)
    y = y * g
    return y * (1.0 + sc) + sh


def _norm_mod_matmul_kernel(x_ref, g_ref, sh_ref, sc_ref, w_ref, o_ref, h_scr):
    @pl.when(pl.program_id(1) == 0)
    def _():
        h_scr[...] = _norm_mod(x_ref[...], g_ref[...], sh_ref[0], sc_ref[0]).astype(bf16)

    o_ref[...] = _dot(h_scr[...], w_ref[...])


def _norm_mod_matmul_h_kernel(x_ref, g_ref, sh_ref, sc_ref, w_ref, o_ref, h_ref):
    @pl.when(pl.program_id(1) == 0)
    def _():
        h_ref[...] = _norm_mod(x_ref[...], g_ref[...], sh_ref[0], sc_ref[0]).astype(bf16)

    o_ref[...] = _dot(h_ref[...], w_ref[...])


def norm_mod_matmul(x, g, mod3, row_of_tile, k_shift, k_scale, w, tm, tn, emit_h):
    ntok, d = x.shape
    n = w.shape[1]

    def mod_spec(k):
        return pl.BlockSpec((1, 1, d), lambda i, j: (row_of_tile(i, tm) * 6 + k, 0, 0))

    in_specs = [pl.BlockSpec((tm, d), lambda i, j: (i, 0)),
                pl.BlockSpec((1, d), lambda i, j: (0, 0)),
                mod_spec(k_shift), mod_spec(k_scale),
                pl.BlockSpec((d, tn), lambda i, j: (0, j))]
    o_spec = pl.BlockSpec((tm, tn), lambda i, j: (i, j))
    if emit_h:
        return pl.pallas_call(
            _norm_mod_matmul_h_kernel,
            grid=(ntok // tm, n // tn),
            in_specs=in_specs,
            out_specs=[o_spec, pl.BlockSpec((tm, d), lambda i, j: (i, 0))],
            out_shape=[jax.ShapeDtypeStruct((ntok, n), f32), jax.ShapeDtypeStruct((ntok, d), bf16)],
            compiler_params=_params("parallel", "arbitrary"),
            name="norm_mod_matmul_h",
        )(x, g.reshape(1, d), mod3, mod3, w)
    return pl.pallas_call(
        _norm_mod_matmul_kernel,
        grid=(ntok // tm, n // tn),
        in_specs=in_specs,
        out_specs=o_spec,
        out_shape=jax.ShapeDtypeStruct((ntok, n), f32),
        scratch_shapes=[pltpu.VMEM((tm, d), bf16)],
        compiler_params=_params("parallel", "arbitrary"),
        name="norm_mod_matmul",
    )(x, g.reshape(1, d), mod3, mod3, w)


def _softmax_pv(logits, values, sink):
    m = sink
    for lg in logits:
        m = jnp.maximum(m, jnp.max(lg, axis=-1, keepdims=True))
    den = jnp.exp(sink - m)
    acc = None
    for lg, v in zip(logits, values):
        p = jnp.exp(lg - m)
        den = den + jnp.sum(p, axis=-1, keepdims=True)
        pv = _dot(p.astype(bf16), v)
        acc = pv if acc is None else acc + pv
    return acc / den


def _sink_column(sink_ref, kvh, group, rows):
    return jnp.concatenate([jnp.full((rows, 1), sink_ref[kvh * group + g], f32) for g in range(group)], axis=0)


def _ctx_attn_kernel(sink_ref, q_ref, k_ref, v_ref, o_ref, *, group, hd, n_kv):
    scale = hd ** -0.5
    for kvh in range(n_kv):
        k = k_ref[:, kvh * hd:(kvh + 1) * hd].astype(bf16)
        v = v_ref[:, kvh * hd:(kvh + 1) * hd].astype(bf16)
        for g in range(group):
            cols = slice((kvh * group + g) * hd, (kvh * group + g + 1) * hd)
            lg = _dot_nt(q_ref[:, cols].astype(bf16), k) * scale
            o = _softmax_pv([lg], [v], sink_ref[kvh * group + g])
            o_ref[:, cols] = o.astype(o_ref.dtype)


def context_attention(proj, sink, nb, seq, n_heads, n_kv, hd):
    group = n_heads // n_kv
    q_w = n_heads * hd
    kv_w = n_kv * hd
    assert q_w % kv_w == 0
    kcol = q_w // kv_w
    return pl.pallas_call(
        functools.partial(_ctx_attn_kernel, group=group, hd=hd, n_kv=n_kv),
        grid=(nb,),
        in_specs=[pl.BlockSpec(memory_space=pltpu.SMEM),
                  pl.BlockSpec((seq, q_w), lambda b: (b, 0)),
                  pl.BlockSpec((seq, kv_w), lambda b: (b, kcol)),
                  pl.BlockSpec((seq, kv_w), lambda b: (b, kcol + 1))],
        out_specs=pl.BlockSpec((seq, q_w), lambda b: (b, 0)),
        out_shape=jax.ShapeDtypeStruct((nb * seq, q_w), bf16),
        compiler_params=_params("parallel"),
        name="context_attention",
    )(sink, proj, proj, proj)


def _rope(x, cos, sin, first_half):
    rot = jnp.where(first_half, -pltpu.roll(x, 3 * (LANES // 4), 1), pltpu.roll(x, LANES // 4, 1))
    return x * cos + rot * sin


def _lat_attn_kernel(sink_ref, q_ref, kp_ref, kc_ref, kn_ref, vp_ref, vc_ref, vn_ref, kx_ref, vx_ref,
                     cp_ref, cc_ref, cn_ref, sp_ref, sc_ref, sn_ref, o_ref, *, group, hd, n_kv):
    n = pl.program_id(1)
    nblk = pl.num_programs(1)
    scale = hd ** -0.5
    blk = (ATT_BLOCK, hd)
    lane = lax.broadcasted_iota(jnp.int32, blk, 1)
    first_half = (lane % (hd // 2)) < (hd // 4)
    row = lax.broadcasted_iota(jnp.int32, (group * ATT_BLOCK, ATT_BLOCK), 0) % ATT_BLOCK
    col = lax.broadcasted_iota(jnp.int32, (group * ATT_BLOCK, ATT_BLOCK), 1)
    mask_p = jnp.logical_and(col >= row, n > 0)
    mask_n = jnp.logical_and(col <= row, n < nblk - 1)
    for kvh in range(n_kv):
        ks = slice(kvh * hd, (kvh + 1) * hd)
        kp = _rope(kp_ref[:, ks], cp_ref[...], sp_ref[...], first_half).astype(bf16)
        kc = _rope(kc_ref[:, ks], cc_ref[...], sc_ref[...], first_half).astype(bf16)
        kn = _rope(kn_ref[:, ks], cn_ref[...], sn_ref[...], first_half).astype(bf16)
        kx = kx_ref[0, :, ks].astype(bf16)
        values = [vp_ref[:, ks].astype(bf16), vc_ref[:, ks].astype(bf16), vn_ref[:, ks].astype(bf16),
                  vx_ref[0, :, ks].astype(bf16)]
        q0 = kvh * group * hd
        q = jnp.concatenate([_rope(q_ref[:, q0 + g * hd:q0 + (g + 1) * hd], cc_ref[...], sc_ref[...],
                                   first_half).astype(bf16) for g in range(group)], axis=0)
        lp = jnp.where(mask_p, _dot_nt(q, kp) * scale, NEG_BIG)
        lc = _dot_nt(q, kc) * scale
        ln = jnp.where(mask_n, _dot_nt(q, kn) * scale, NEG_BIG)
        lx = _dot_nt(q, kx) * scale
        o = _softmax_pv([lp, lc, ln, lx], values, _sink_column(sink_ref, kvh, group, ATT_BLOCK))
        for g in range(group):
            o_ref[:, q0 + g * hd:q0 + (g + 1) * hd] = o[g * ATT_BLOCK:(g + 1) * ATT_BLOCK].astype(o_ref.dtype)


def latent_attention(proj, sink, k_ctx, v_ctx, cos, sin, nb, seq, n_heads, n_kv, hd):
    assert WINDOW == ATT_BLOCK
    group = n_heads // n_kv
    q_w = n_heads * hd
    kv_w = n_kv * hd
    assert q_w % kv_w == 0
    kcol = q_w // kv_w
    vcol = kcol + 1
    nblk = seq // ATT_BLOCK
    past = k_ctx.shape[1]

    def rb(b, n):
        return b * nblk + n

    def prev(n):
        return jnp.maximum(n - 1, 0)

    def nxt(n):
        return jnp.minimum(n + 1, nblk - 1)

    def kv_spec(col, f):
        return pl.BlockSpec((ATT_BLOCK, kv_w), lambda b, n: (rb(b, f(n)), col))

    def tab_spec(f):
        return pl.BlockSpec((ATT_BLOCK, hd), lambda b, n: (f(n), 0))

    ident = lambda n: n
    ctx_spec = pl.BlockSpec((1, past, kv_w), lambda b, n: (b, 0, 0))
    return pl.pallas_call(
        functools.partial(_lat_attn_kernel, group=group, hd=hd, n_kv=n_kv),
        grid=(nb, nblk),
        in_specs=[pl.BlockSpec(memory_space=pltpu.SMEM),
                  pl.BlockSpec((ATT_BLOCK, q_w), lambda b, n: (rb(b, n), 0)),
                  kv_spec(kcol, prev), kv_spec(kcol, ident), kv_spec(kcol, nxt),
                  kv_spec(vcol, prev), kv_spec(vcol, ident), kv_spec(vcol, nxt),
                  ctx_spec, ctx_spec,
                  tab_spec(prev), tab_spec(ident), tab_spec(nxt),
                  tab_spec(prev), tab_spec(ident), tab_spec(nxt)],
        out_specs=pl.BlockSpec((ATT_BLOCK, q_w), lambda b, n: (rb(b, n), 0)),
        out_shape=jax.ShapeDtypeStruct((nb * seq, q_w), bf16),
        compiler_params=_params("parallel", "parallel"),
        name="latent_attention",
    )(sink, proj, proj, proj, proj, proj, proj, proj, k_ctx, v_ctx, cos, cos, cos, sin, sin, sin)


RET_HEADS_PER_STEP_CHOICES = (4, 2)


def _ret_kernel(lg_ref, q_ref, k_ref, v_ref, gf_ref, gb_ref, s0f_ref, s0b_ref, o_ref, sf_ref, sb_ref,
                of_scr, ob_scr, *, n_chunks, dk, hps):
    hb = pl.program_id(1)
    c = RET_CHUNK
    ii = lax.broadcasted_iota(jnp.int32, (c, c), 0).astype(f32)
    jj = lax.broadcasted_iota(jnp.int32, (c, c), 1).astype(f32)
    d = ii - jj
    ri = lax.broadcasted_iota(jnp.int32, (c, dk), 0).astype(f32)
    kscale = dk ** -0.5

    consts = []
    for hh in range(hps):
        lgf = lg_ref[0, hb * hps + hh]
        lgb = lg_ref[1, hb * hps + hh]
        fwd = (jnp.where(d >= 0, jnp.exp(jnp.maximum(d, 0.0) * lgf), 0.0),
               jnp.exp((ri + 1.0) * lgf), jnp.exp((c - 1.0 - ri) * lgf), jnp.exp(jnp.full((dk, dk), c * lgf, f32)))
        bwd = (jnp.where(d <= 0, jnp.exp(jnp.maximum(-d, 0.0) * lgb), 0.0),
               jnp.exp((c - ri) * lgb), jnp.exp(ri * lgb), jnp.exp(jnp.full((dk, dk), c * lgb, f32)))
        consts.append((fwd, bwd))

    def body(n, states):
        chains = []
        for hh in range(hps):
            for direction, (chunk_idx, o_scr) in enumerate(((n, of_scr), (n_chunks - 1 - n, ob_scr))):
                decay, qdec, kdec, cd = consts[hh][direction]
                r = pl.ds(pl.multiple_of(chunk_idx * c, c), c)
                cs = slice(hh * dk, (hh + 1) * dk)
                q = q_ref[r, cs]
                k = k_ref[r, cs] * kscale
                vb = v_ref[r, cs].astype(bf16)
                state = states[hh][direction]
                raw = _dot_nt(q.astype(bf16), k.astype(bf16))
                cross = _dot((q * qdec).astype(bf16), state.astype(bf16))
                kv = _dot((k * kdec).T.astype(bf16), vb)
                chains.append((raw, cross, kv, vb, decay, cd, state, o_scr, r, cs))
        scores = [(raw * decay).astype(bf16) for raw, _, _, _, decay, *_ in chains]
        new = []
        for (raw, cross, kv, vb, decay, cd, state, o_scr, r, cs), sc in zip(chains, scores):
            o_scr[r, cs] = _dot(sc, vb) + cross
            new.append(cd * state + kv)
        return tuple((new[2 * hh], new[2 * hh + 1]) for hh in range(hps))

    init = tuple((s0f_ref[0, hh], s0b_ref[0, hh]) for hh in range(hps))
    final = lax.fori_loop(0, n_chunks, body, init)
    for hh in range(hps):
        sf_ref[0, hh] = final[hh][0]
        sb_ref[0, hh] = final[hh][1]

    def hn(o):
        return o * lax.rsqrt(jnp.mean(o * o, axis=-1, keepdims=True) + NORM_EPS)

    def finish(n, carry):
        r = pl.ds(pl.multiple_of(n * c, c), c)
        for hh in range(hps):
            cs = slice(hh * dk, (hh + 1) * dk)
            gf = gf_ref[r, cs]
            gb = gb_ref[r, cs]
            out = hn(of_scr[r, cs]) * (gf * jax.nn.sigmoid(gf)) + hn(ob_scr[r, cs]) * (gb * jax.nn.sigmoid(gb))
            o_ref[r, cs] = out.astype(o_ref.dtype)
        return carry

    lax.fori_loop(0, n_chunks, finish, 0)


def retention(proj, log_gamma, s0f, s0b, nb, seq, col0, n_heads, dk):
    n_chunks = seq // RET_CHUNK
    hps = 1
    for cand in RET_HEADS_PER_STEP_CHOICES:
        if n_heads % cand == 0 and seq * cand * dk * 4 * 12 <= VMEM_LIMIT_BYTES // 2:
            hps = cand
            break
    w = hps * dk
    c0 = col0 // w
    gpb = n_heads // hps

    def col_spec(k):
        return pl.BlockSpec((seq, w), lambda b, h: (b, c0 + k * gpb + h))

    st_spec = pl.BlockSpec((1, hps, dk, dk), lambda b, h: (b, h, 0, 0))
    st_shape = jax.ShapeDtypeStruct((nb, n_heads, dk, dk), f32)
    return pl.pallas_call(
        functools.partial(_ret_kernel, n_chunks=n_chunks, dk=dk, hps=hps),
        grid=(nb, gpb),
        in_specs=[pl.BlockSpec(memory_space=pltpu.SMEM),
                  col_spec(0), col_spec(1), col_spec(2), col_spec(3), col_spec(4),
                  st_spec, st_spec],
        out_specs=[pl.BlockSpec((seq, w), lambda b, h: (b, h)), st_spec, st_spec],
        out_shape=[jax.ShapeDtypeStruct((nb * seq, n_heads * dk), bf16), st_shape, st_shape],
        scratch_shapes=[pltpu.VMEM((seq, w), f32), pltpu.VMEM((seq, w), f32)],
        compiler_params=_params("parallel", "parallel"),
        name="retention",
    )(log_gamma, proj, proj, proj, proj, proj, s0f, s0b)


def _merge_kernel(oa_ref, or_ref, wa_ref, wr_ref, ga_ref, gr_ref, o_ref):
    a = _dot(oa_ref[...], wa_ref[...])
    r = _dot(or_ref[...], wr_ref[...])
    o_ref[...] = (jax.nn.sigmoid(ga_ref[...]) * a + jax.nn.sigmoid(gr_ref[...]) * r).astype(o_ref.dtype)


def merge_branches(o_att, o_ret, w_pa, w_pr, proj, gate_col0, tm, tn):
    ntok, qw = o_att.shape
    rw = o_ret.shape[1]
    d = w_pa.shape[1]
    ga0 = gate_col0 // tn
    gr0 = (gate_col0 + d) // tn
    return pl.pallas_call(
        _merge_kernel,
        grid=(ntok // tm, d // tn),
        in_specs=[pl.BlockSpec((tm, qw), lambda i, j: (i, 0)),
                  pl.BlockSpec((tm, rw), lambda i, j: (i, 0)),
                  pl.BlockSpec((qw, tn), lambda i, j: (0, j)),
                  pl.BlockSpec((rw, tn), lambda i, j: (0, j)),
                  pl.BlockSpec((tm, tn), lambda i, j: (i, ga0 + j)),
                  pl.BlockSpec((tm, tn), lambda i, j: (i, gr0 + j))],
        out_specs=pl.BlockSpec((tm, tn), lambda i, j: (i, j)),
        out_shape=jax.ShapeDtypeStruct((ntok, d), bf16),
        compiler_params=_params("parallel", "arbitrary"),
        name="merge_branches",
    )(o_att, o_ret, w_pa, w_pr, proj, proj)


def _out_proj_kernel(m_ref, w_ref, x_ref, g_ref, o_ref):
    o_ref[...] = x_ref[...] + g_ref[0] * _dot(m_ref[...], w_ref[...])


def out_projection(merged, w_out, x, mod3, row_of_tile, k_gate, tm, tn):
    ntok, d = x.shape
    kd = merged.shape[1]
    return pl.pallas_call(
        _out_proj_kernel,
        grid=(ntok // tm, d // tn),
        in_specs=[pl.BlockSpec((tm, kd), lambda i, j: (i, 0)),
                  pl.BlockSpec((kd, tn), lambda i, j: (0, j)),
                  pl.BlockSpec((tm, tn), lambda i, j: (i, j)),
                  pl.BlockSpec((1, 1, tn), lambda i, j: (row_of_tile(i, tm) * 6 + k_gate, 0, j))],
        out_specs=pl.BlockSpec((tm, tn), lambda i, j: (i, j)),
        out_shape=jax.ShapeDtypeStruct((ntok, d), f32),
        compiler_params=_params("parallel", "arbitrary"),
        name="out_projection",
    )(merged, w_out, x, mod3)


def _extract_round(r, x_ref, rank_ref, vals_ref, pos, sentinel):
    x = x_ref[...]
    m = jnp.max(x, axis=0, keepdims=True)
    idx = jnp.min(jnp.where(x == m, pos, sentinel), axis=0, keepdims=True)
    sel = pos == idx
    rank_ref[...] = jnp.where(sel, lax.convert_element_type(r, f32), rank_ref[...])
    x_ref[...] = jnp.where(sel, -jnp.inf, x)
    vals_ref[pl.ds(r, 1), :] = m


def _candidate_layout(k):
    slabs = []
    for a in range(k // 2):
        n_valid = k // (a + 1)
        slabs.append((a, n_valid, -(-n_valid // SUBLANES) * SUBLANES))
    return slabs


def _peer_route_kernel(q_ref, keys_ref, n1_ref, e1_ref, r2_ref, e2_ref, x1_scr, x2_scr, v1_scr, v2_scr,
                       c_scr, crank_scr, cv_scr, rank2_scr, e2_scr, *, half):
    k = PEER_TOPK
    n_keys, t = x1_scr.shape
    s1 = _dot_nt(keys_ref[0, 0].astype(bf16), q_ref[:, :half].astype(bf16))
    s2 = _dot_nt(keys_ref[0, 1].astype(bf16), q_ref[:, half:].astype(bf16))
    x1_scr[...] = s1
    x2_scr[...] = s2
    e1_ref[0] = jnp.exp(s1 - jnp.max(s1, axis=0, keepdims=True))
    e2_scr[...] = jnp.exp(s2 - jnp.max(s2, axis=0, keepdims=True))
    n1_ref[0] = jnp.full((n_keys, t), float(k), f32)
    rank2_scr[...] = jnp.full((n_keys, t), float(k), f32)
    key_pos = lax.broadcasted_iota(jnp.int32, (n_keys, t), 0).astype(f32)

    def sub_round(r, carry):
        _extract_round(r, x1_scr, n1_ref.at[0], v1_scr, key_pos, float(n_keys))
        _extract_round(r, x2_scr, rank2_scr, v2_scr, key_pos, float(n_keys))
        return carry

    lax.fori_loop(0, k, sub_round, 0)

    v1 = v1_scr[...]
    v2 = v2_scr[...]
    slabs = _candidate_layout(k)
    cands, poss = [], []
    for a, n_valid, n_rows in slabs:
        b = lax.broadcasted_iota(jnp.int32, (n_rows, t), 0)
        cands.append(jnp.where(b < n_valid, v1[a:a + 1, :] + v2[:n_rows, :], -jnp.inf))
        poss.append((a * k + b).astype(f32))
    a_hi = lax.broadcasted_iota(jnp.int32, (k // 2, t), 0) + k // 2
    cands.append(v1[k // 2:, :] + v2[0:1, :])
    poss.append((a_hi * k).astype(f32))
    c_scr[...] = jnp.concatenate(cands, axis=0)
    cand_pos = jnp.concatenate(poss, axis=0)
    crank_scr[...] = jnp.full(c_scr.shape, float(k), f32)

    def cand_round(r, carry):
        _extract_round(r, c_scr, crank_scr, cv_scr, cand_pos, float(k * k))
        return carry

    lax.fori_loop(0, k, cand_round, 0)

    picked = jnp.where(crank_scr[...] < float(k), 1.0, 0.0)
    cnt = []
    row0 = 0
    for a, n_valid, n_rows in slabs:
        cnt.append(jnp.sum(picked[row0:row0 + n_rows, :], axis=0, keepdims=True))
        row0 += n_rows
    for a in range(k // 2, k):
        cnt.append(picked[row0 + a - k // 2:row0 + a - k // 2 + 1, :])
    cvals = cv_scr[...]
    z = jnp.sum(jnp.exp(cvals - cvals[0:1, :]), axis=0, keepdims=True)
    rank1 = n1_ref[0]
    n1 = jnp.zeros_like(rank1)
    for a in range(k):
        n1 = jnp.where(rank1 == float(a), cnt[a], n1)
    n1_ref[0] = n1
    r2_ref[0] = rank2_scr[...].astype(bf16)
    e2_ref[0] = (e2_scr[...] / z).astype(bf16)


def peer_route(qp, sub_keys, t_tile):
    ntok = qp.shape[0]
    heads, _, n_keys, half = sub_keys.shape
    k = PEER_TOPK
    n_cand = sum(s[2] for s in _candidate_layout(k)) + k // 2
    out_spec = pl.BlockSpec((1, n_keys, t_tile), lambda i, h: (h, 0, i))
    out_shape = jax.ShapeDtypeStruct((heads, n_keys, ntok), f32)
    out_shape_b = jax.ShapeDtypeStruct((heads, n_keys, ntok), bf16)
    return pl.pallas_call(
        functools.partial(_peer_route_kernel, half=half),
        grid=(ntok // t_tile, heads),
        in_specs=[pl.BlockSpec((t_tile, 2 * half), lambda i, h: (i, h)),
                  pl.BlockSpec((1, 2, n_keys, half), lambda i, h: (h, 0, 0, 0))],
        out_specs=[out_spec] * 4,
        out_shape=[out_shape, out_shape, out_shape_b, out_shape_b],
        scratch_shapes=[pltpu.VMEM((n_keys, t_tile), f32), pltpu.VMEM((n_keys, t_tile), f32),
                        pltpu.VMEM((k, t_tile), f32), pltpu.VMEM((k, t_tile), f32),
                        pltpu.VMEM((n_cand, t_tile), f32), pltpu.VMEM((n_cand, t_tile), f32),
                        pltpu.VMEM((k, t_tile), f32),
                        pltpu.VMEM((n_keys, t_tile), f32), pltpu.VMEM((n_keys, t_tile), f32)],
        compiler_params=_params("parallel", "parallel"),
        name="peer_route",
    )(qp, sub_keys)


BF16_SUBLANES = 2 * SUBLANES


def _bcast_rows_bf16(row, n_rows):
    tile = jnp.broadcast_to(row, (BF16_SUBLANES, row.shape[1])).astype(bf16)
    return jnp.concatenate([tile] * (n_rows // BF16_SUBLANES), axis=0)


def _peer_dense_kernel(ht_ref, u_ref, v_ref, n1_ref, e1_ref, r2_ref, e2_ref, acc_ref, coef_scr, r2_scr, e2_scr,
                       *, heads, n_keys):
    e = pl.program_id(1)
    et = u_ref.shape[0]
    t = ht_ref.shape[1]
    cpt = et // n_keys

    @pl.when(e == 0)
    def _():
        acc_ref[...] = jnp.zeros_like(acc_ref)
        r2_scr[...] = r2_ref[...]
        e2_scr[...] = e2_ref[...]

    act = _dot(u_ref[...], ht_ref[...])
    for c in range(cpt):
        i = e * cpt + c
        n1_rows = [n1_ref[h, pl.ds(i, 1), :] for h in range(heads)]
        e1_rows = [e1_ref[h, pl.ds(i, 1), :] for h in range(heads)]
        for tk in range(t // LANES):
            cols = slice(tk * LANES, (tk + 1) * LANES)
            gate = None
            for h in range(heads):
                n1 = _bcast_rows_bf16(n1_rows[h][:, cols], n_keys)
                e1 = _bcast_rows_bf16(e1_rows[h][:, cols], n_keys)
                term = jnp.where(r2_scr[h, :, cols] < n1, e2_scr[h, :, cols] * e1, jnp.zeros((), bf16))
                gate = term if gate is None else gate + term
            a = act[c * n_keys:(c + 1) * n_keys, cols].astype(bf16)
            gelu = 0.5 * a * (1.0 + lax.erf(a * (2.0 ** -0.5)))
            coef_scr[c, cols, :] = (gate * gelu).T
    lhs = jnp.concatenate([coef_scr[c] for c in range(cpt)], axis=-1)
    acc_ref[...] += _dot(lhs, v_ref[...])


def peer_dense(h_t, u, v, n1, e1, r2, e2, t_tile, e_tile):
    d, ntok = h_t.shape
    n_exp = u.shape[0]
    heads, n_keys, _ = n1.shape
    once = pl.Buffered(1)
    route_spec = pl.BlockSpec((heads, n_keys, t_tile), lambda t, e: (0, 0, t), pipeline_mode=once)
    return pl.pallas_call(
        functools.partial(_peer_dense_kernel, heads=heads, n_keys=n_keys),
        grid=(ntok // t_tile, n_exp // e_tile),
        in_specs=[pl.BlockSpec((d, t_tile), lambda t, e: (0, t), pipeline_mode=once),
                  pl.BlockSpec((e_tile, d), lambda t, e: (e, 0)),
                  pl.BlockSpec((e_tile, d), lambda t, e: (e, 0)),
                  route_spec, route_spec, route_spec, route_spec],
        out_specs=pl.BlockSpec((t_tile, d), lambda t, e: (t, 0), pipeline_mode=once),
        out_shape=jax.ShapeDtypeStruct((ntok, d), f32),
        scratch_shapes=[pltpu.VMEM((e_tile // n_keys, t_tile, n_keys), bf16),
                        pltpu.VMEM((heads, n_keys, t_tile), bf16),
                        pltpu.VMEM((heads, n_keys, t_tile), bf16)],
        compiler_params=_params("parallel", "arbitrary"),
        name="peer_dense",
    )(h_t, u, v, n1, e1, r2, e2)


def _residual_kernel(x_ref, p_ref, g_ref, o_ref):
    o_ref[...] = x_ref[...] + g_ref[0] * p_ref[...]


def _residual_norm_kernel(x_ref, p_ref, g_ref, w_ref, o_ref):
    x = x_ref[...] + g_ref[0] * p_ref[...]
    y = x * lax.rsqrt(jnp.mean(x * x, axis=-1, keepdims=True) + NORM_EPS)
    o_ref[...] = y * w_ref[...]


def gated_residual(x, p, mod3, row_of_tile, k_gate, tm, norm_w=None):
    ntok, d = x.shape
    row_spec = pl.BlockSpec((tm, d), lambda i: (i, 0))
    in_specs = [row_spec, row_spec, pl.BlockSpec((1, 1, d), lambda i: (row_of_tile(i, tm) * 6 + k_gate, 0, 0))]
    args = [x, p, mod3]
    kern = _residual_kernel
    if norm_w is not None:
        in_specs.append(pl.BlockSpec((1, d), lambda i: (0, 0)))
        args.append(norm_w.reshape(1, d))
        kern = _residual_norm_kernel
    return pl.pallas_call(
        kern,
        grid=(ntok // tm,),
        in_specs=in_specs,
        out_specs=row_spec,
        out_shape=jax.ShapeDtypeStruct((ntok, d), f32),
        compiler_params=_params("parallel"),
        name="gated_residual",
    )(*args)


def _axial_rope_tables(n_tokens, hd):
    rows = n_tokens // GRID_W
    row = jnp.repeat(jnp.arange(rows), GRID_W).astype(f32)
    col = jnp.tile(jnp.arange(GRID_W), rows).astype(f32)
    half = hd // 2
    inv = 1.0 / (ROPE_BASE ** (jnp.arange(0, half, 2, dtype=f32) / half))
    ar = row[:, None] * inv[None, :]
    ac = col[:, None] * inv[None, :]
    ang = jnp.concatenate([ar, ar, ac, ac], axis=-1)
    return jnp.cos(ang), jnp.sin(ang)


def kernel(x_prompt, x_sample, cache_attn_k, cache_attn_v, state_ret_fwd, state_ret_bwd, c, c_ctx, w_ada, b_ada,
           norm_mix, norm_ffn, w_in, attn_sink, ret_decay_logit, w_proj_attn, w_proj_ret, w_out, peer_w_query,
           peer_sub_keys, peer_u, peer_v, norm_final):
    batch, seq, d = x_prompt.shape
    dbatch, dseq, _ = x_sample.shape
    depth = w_in.shape[0]
    assert depth >= 1
    n_heads = attn_sink.shape[1]
    _, _, past, n_kv, hd = cache_attn_k.shape
    ret_heads, ret_dk, ret_dv = state_ret_fwd.shape[2:]
    assert ret_dk == ret_dv
    peer_heads, peer_dkey = peer_w_query.shape[2:]
    q_w = n_heads * hd
    kv_w = n_kv * hd
    r_w = ret_heads * ret_dk
    ret_col0 = q_w + 2 * kv_w
    gate_col0 = ret_col0 + 5 * r_w

    n_prompt = batch * seq
    n_lat = dbatch * dseq

    def cond_row_ctx(i, tm):
        return 0

    def cond_row_lat(i, tm):
        return 1 + i // (dseq // tm)

    n_cond = -(-(dbatch + 1) // SUBLANES) * SUBLANES
    cond = jnp.zeros((n_cond, d), f32).at[0].set(c_ctx).at[1:dbatch + 1].set(c)
    cos, sin = _axial_rope_tables(dseq, hd)
    tm_small = _tile(TM_PROJ, n_prompt, dseq)
    tm_big = _tile(TM_MERGE, n_prompt, dseq)
    tm_res = _tile(TM_RESIDUAL, n_prompt, dseq)
    tn_in = _tile(TN_PROJ, w_in.shape[2])
    tn_gate = _tile(TN_MERGE, d, gate_col0)
    tn_q = _tile(TN_PROJ, peer_heads * peer_dkey)
    e_tile = _tile(E_DENSE, peer_u.shape[1])

    xp = x_prompt.reshape(n_prompt, d)
    xs = x_sample.reshape(n_lat, d)
    new_k, new_v, new_sf, new_sb = [], [], [], []
    for l in range(depth):
        last = l == depth - 1
        mod3 = ada_modulation(cond, w_ada[l], b_ada[l]).reshape(n_cond * 6, 1, d)
        sink = attn_sink[l].astype(f32)
        log_gamma = jax.nn.log_sigmoid(ret_decay_logit[l].astype(f32))
        w_in_b = w_in[l].astype(bf16)
        w_pa_b = w_proj_attn[l].astype(bf16)
        w_pr_b = w_proj_ret[l].astype(bf16)
        w_out_b = w_out[l].astype(bf16)
        wq_b = peer_w_query[l].reshape(d, peer_heads * peer_dkey).astype(bf16)
        u_b = peer_u[l].astype(bf16)
        v_b = peer_v[l].astype(bf16)

        def channel_mix(x, proj, o_att, o_ret, cond_row):
            merged = merge_branches(o_att, o_ret, w_pa_b, w_pr_b, proj, gate_col0, tm_big, tn_gate)
            x = out_projection(merged, w_out_b, x, mod3, cond_row, 2, tm_big, tn_gate)
            qp, h2 = norm_mod_matmul(x, norm_ffn[l], mod3, cond_row, 3, 4, wq_b, tm_small, tn_q, True)
            n1, e1, r2, e2 = peer_route(qp, peer_sub_keys[l], _tile(T_ROUTE, x.shape[0]))
            p = peer_dense(h2.T, u_b, v_b, n1, e1, r2, e2, _tile(T_DENSE, x.shape[0]), e_tile)
            return gated_residual(x, p, mod3, cond_row, 5, tm_res, norm_final if last else None)

        proj = norm_mod_matmul(xp, norm_mix[l], mod3, cond_row_ctx, 0, 1, w_in_b, tm_small, tn_in, False)
        o_att = context_attention(proj, sink, batch, seq, n_heads, n_kv, hd)
        zero_state = jnp.zeros((batch, ret_heads, ret_dk, ret_dv), f32)
        o_ret, s_f, s_b = retention(proj, log_gamma, zero_state, zero_state, batch, seq, ret_col0, ret_heads, ret_dk)
        new_k.append(proj[:, q_w:q_w + kv_w].reshape(batch, seq, n_kv, hd))
        new_v.append(proj[:, q_w + kv_w:q_w + 2 * kv_w].reshape(batch, seq, n_kv, hd))
        new_sf.append(s_f)
        new_sb.append(s_b)
        xp = channel_mix(xp, proj, o_att, o_ret, cond_row_ctx)

        proj = norm_mod_matmul(xs, norm_mix[l], mod3, cond_row_lat, 0, 1, w_in_b, tm_small, tn_in, False)
        o_att = latent_attention(proj, sink, cache_attn_k[:, l].reshape(dbatch, past, kv_w),
                                 cache_attn_v[:, l].reshape(dbatch, past, kv_w), cos, sin,
                                 dbatch, dseq, n_heads, n_kv, hd)
        o_ret, _, _ = retention(proj, log_gamma, state_ret_fwd[:, l], state_ret_bwd[:, l], dbatch, dseq,
                                ret_col0, ret_heads, ret_dk)
        xs = channel_mix(xs, proj, o_att, o_ret, cond_row_lat)

    return (xp.reshape(batch, seq, d), xs.reshape(dbatch, dseq, d), jnp.stack(new_k, axis=1),
            jnp.stack(new_v, axis=1), jnp.stack(new_sf, axis=1), jnp.stack(new_sb, axis=1))
```

```python
import functools

import jax
import jax.numpy as jnp
from jax import lax
from jax.experimental import pallas as pl
from jax.experimental.pallas import tpu as pltpu

GRID_W = 64
WINDOW = 128
ATT_BLOCK = 128
ROPE_BASE = 10000.0
RET_CHUNK = 128
PEER_TOPK = 16
NORM_EPS = 1e-6

LANES = 128
SUBLANES = 8
VMEM_LIMIT_BYTES = 56 * 1024 * 1024
NEG_BIG = -1e30

TM_PROJ, TN_PROJ = 512, 1024
TM_MERGE, TN_MERGE = 1024, 512
TM_RESIDUAL = 256
T_ROUTE = 2048
T_DENSE, E_DENSE = 512, 1024

f32 = jnp.float32
bf16 = jnp.bfloat16


def _params(*sem):
    return pltpu.CompilerParams(dimension_semantics=sem, vmem_limit_bytes=VMEM_LIMIT_BYTES)


def _tile(pref, *dims):
    t = pref
    while any(dim % t for dim in dims):
        t //= 2
    assert t >= LANES
    return t


def _dot(a, b):
    return jnp.dot(a, b, preferred_element_type=f32)


def _dot_nt(a, b):
    return lax.dot_general(a, b, (((1,), (1,)), ((), ())), preferred_element_type=f32)


def _ada_kernel(cond_ref, w_ref, b_ref, o_ref):
    c = cond_ref[...]
    s = (c * jax.nn.sigmoid(c)).astype(bf16)
    o_ref[...] = _dot(s, w_ref[...].astype(bf16)) + b_ref[...]


def ada_modulation(cond, w_ada, b_ada):
    r, d = cond.shape
    n = w_ada.shape[1]
    tn = _tile(TN_MERGE, n)
    return pl.pallas_call(
        _ada_kernel,
        grid=(n // tn,),
        in_specs=[pl.BlockSpec((r, d), lambda j: (0, 0)),
                  pl.BlockSpec((d, tn), lambda j: (0, j)),
                  pl.BlockSpec((1, tn), lambda j: (0, j))],
        out_specs=pl.BlockSpec((r, tn), lambda j: (0, j)),
        out_shape=jax.ShapeDtypeStruct((r, n), f32),
        compiler_params=_params("parallel"),
        name="ada_modulation",
    )(cond, w_ada, b_ada.reshape(1, n))


def _norm_mod(x, g, sh, sc):
    y = x * lax.rsqrt(jnp.mean(x * x, axis=-1, keepdims=True) + NORM_EPS)
    y = y * g
    return y * (1.0 + sc) + sh


def _norm_mod_matmul_kernel(x_ref, g_ref, sh_ref, sc_ref, w_ref, o_ref, h_scr):
    @pl.when(pl.program_id(1) == 0)
    def _():
        h_scr[...] = _norm_mod(x_ref[...], g_ref[...], sh_ref[0], sc_ref[0]).astype(bf16)

    o_ref[...] = _dot(h_scr[...], w_ref[...])


def _norm_mod_matmul_h_kernel(x_ref, g_ref, sh_ref, sc_ref, w_ref, o_ref, h_ref):
    @pl.when(pl.program_id(1) == 0)
    def _():
        h_ref[...] = _norm_mod(x_ref[...], g_ref[...], sh_ref[0], sc_ref[0]).astype(bf16)

    o_ref[...] = _dot(h_ref[...], w_ref[...])


def norm_mod_matmul(x, g, mod3, row_of_tile, k_shift, k_scale, w, tm, tn, emit_h):
    ntok, d = x.shape
    n = w.shape[1]

    def mod_spec(k):
        return pl.BlockSpec((1, 1, d), lambda i, j: (row_of_tile(i, tm) * 6 + k, 0, 0))

    in_specs = [pl.BlockSpec((tm, d), lambda i, j: (i, 0)),
                pl.BlockSpec((1, d), lambda i, j: (0, 0)),
                mod_spec(k_shift), mod_spec(k_scale),
                pl.BlockSpec((d, tn), lambda i, j: (0, j))]
    o_spec = pl.BlockSpec((tm, tn), lambda i, j: (i, j))
    if emit_h:
        return pl.pallas_call(
            _norm_mod_matmul_h_kernel,
            grid=(ntok // tm, n // tn),
            in_specs=in_specs,
            out_specs=[o_spec, pl.BlockSpec((tm, d), lambda i, j: (i, 0))],
            out_shape=[jax.ShapeDtypeStruct((ntok, n), f32), jax.ShapeDtypeStruct((ntok, d), bf16)],
            compiler_params=_params("parallel", "arbitrary"),
            name="norm_mod_matmul_h",
        )(x, g.reshape(1, d), mod3, mod3, w)
    return pl.pallas_call(
        _norm_mod_matmul_kernel,
        grid=(ntok // tm, n // tn),
        in_specs=in_specs,
        out_specs=o_spec,
        out_shape=jax.ShapeDtypeStruct((ntok, n), f32),
        scratch_shapes=[pltpu.VMEM((tm, d), bf16)],
        compiler_params=_params("parallel", "arbitrary"),
        name="norm_mod_matmul",
    )(x, g.reshape(1, d), mod3, mod3, w)


def _softmax_pv(logits, values, sink):
    m = sink
    for lg in logits:
        m = jnp.maximum(m, jnp.max(lg, axis=-1, keepdims=True))
    den = jnp.exp(sink - m)
    acc = None
    for lg, v in zip(logits, values):
        p = jnp.exp(lg - m)
        den = den + jnp.sum(p, axis=-1, keepdims=True)
        pv = _dot(p.astype(bf16), v)
        acc = pv if acc is None else acc + pv
    return acc / den


def _sink_column(sink_ref, kvh, group, rows):
    return jnp.concatenate([jnp.full((rows, 1), sink_ref[kvh * group + g], f32) for g in range(group)], axis=0)


def _ctx_attn_kernel(sink_ref, q_ref, k_ref, v_ref, o_ref, *, group, hd, n_kv):
    scale = hd ** -0.5
    for kvh in range(n_kv):
        k = k_ref[:, kvh * hd:(kvh + 1) * hd].astype(bf16)
        v = v_ref[:, kvh * hd:(kvh + 1) * hd].astype(bf16)
        for g in range(group):
            cols = slice((kvh * group + g) * hd, (kvh * group + g + 1) * hd)
            lg = _dot_nt(q_ref[:, cols].astype(bf16), k) * scale
            o = _softmax_pv([lg], [v], sink_ref[kvh * group + g])
            o_ref[:, cols] = o.astype(o_ref.dtype)


def context_attention(proj, sink, nb, seq, n_heads, n_kv, hd):
    group = n_heads // n_kv
    q_w = n_heads * hd
    kv_w = n_kv * hd
    assert q_w % kv_w == 0
    kcol = q_w // kv_w
    return pl.pallas_call(
        functools.partial(_ctx_attn_kernel, group=group, hd=hd, n_kv=n_kv),
        grid=(nb,),
        in_specs=[pl.BlockSpec(memory_space=pltpu.SMEM),
                  pl.BlockSpec((seq, q_w), lambda b: (b, 0)),
                  pl.BlockSpec((seq, kv_w), lambda b: (b, kcol)),
                  pl.BlockSpec((seq, kv_w), lambda b: (b, kcol + 1))],
        out_specs=pl.BlockSpec((seq, q_w), lambda b: (b, 0)),
        out_shape=jax.ShapeDtypeStruct((nb * seq, q_w), bf16),
        compiler_params=_params("parallel"),
        name="context_attention",
    )(sink, proj, proj, proj)


def _rope(x, cos, sin, first_half):
    rot = jnp.where(first_half, -pltpu.roll(x, 3 * (LANES // 4), 1), pltpu.roll(x, LANES // 4, 1))
    return x * cos + rot * sin


def _lat_attn_kernel(sink_ref, q_ref, kp_ref, kc_ref, kn_ref, vp_ref, vc_ref, vn_ref, kx_ref, vx_ref,
                     cp_ref, cc_ref, cn_ref, sp_ref, sc_ref, sn_ref, o_ref, *, group, hd, n_kv):
    n = pl.program_id(1)
    nblk = pl.num_programs(1)
    scale = hd ** -0.5
    blk = (ATT_BLOCK, hd)
    lane = lax.broadcasted_iota(jnp.int32, blk, 1)
    first_half = (lane % (hd // 2)) < (hd // 4)
    row = lax.broadcasted_iota(jnp.int32, (group * ATT_BLOCK, ATT_BLOCK), 0) % ATT_BLOCK
    col = lax.broadcasted_iota(jnp.int32, (group * ATT_BLOCK, ATT_BLOCK), 1)
    mask_p = jnp.logical_and(col >= row, n > 0)
    mask_n = jnp.logical_and(col <= row, n < nblk - 1)
    for kvh in range(n_kv):
        ks = slice(kvh * hd, (kvh + 1) * hd)
        kp = _rope(kp_ref[:, ks], cp_ref[...], sp_ref[...], first_half).astype(bf16)
        kc = _rope(kc_ref[:, ks], cc_ref[...], sc_ref[...], first_half).astype(bf16)
        kn = _rope(kn_ref[:, ks], cn_ref[...], sn_ref[...], first_half).astype(bf16)
        kx = kx_ref[0, :, ks].astype(bf16)
        values = [vp_ref[:, ks].astype(bf16), vc_ref[:, ks].astype(bf16), vn_ref[:, ks].astype(bf16),
                  vx_ref[0, :, ks].astype(bf16)]
        q0 = kvh * group * hd
        q = jnp.concatenate([_rope(q_ref[:, q0 + g * hd:q0 + (g + 1) * hd], cc_ref[...], sc_ref[...],
                                   first_half).astype(bf16) for g in range(group)], axis=0)
        lp = jnp.where(mask_p, _dot_nt(q, kp) * scale, NEG_BIG)
        lc = _dot_nt(q, kc) * scale
        ln = jnp.where(mask_n, _dot_nt(q, kn) * scale, NEG_BIG)
        lx = _dot_nt(q, kx) * scale
        o = _softmax_pv([lp, lc, ln, lx], values, _sink_column(sink_ref, kvh, group, ATT_BLOCK))
        for g in range(group):
            o_ref[:, q0 + g * hd:q0 + (g + 1) * hd] = o[g * ATT_BLOCK:(g + 1) * ATT_BLOCK].astype(o_ref.dtype)


def latent_attention(proj, sink, k_ctx, v_ctx, cos, sin, nb, seq, n_heads, n_kv, hd):
    assert WINDOW == ATT_BLOCK
    group = n_heads // n_kv
    q_w = n_heads * hd
    kv_w = n_kv * hd
    assert q_w % kv_w == 0
    kcol = q_w // kv_w
    vcol = kcol + 1
    nblk = seq // ATT_BLOCK
    past = k_ctx.shape[1]

    def rb(b, n):
        return b * nblk + n

    def prev(n):
        return jnp.maximum(n - 1, 0)

    def nxt(n):
        return jnp.minimum(n + 1, nblk - 1)

    def kv_spec(col, f):
        return pl.BlockSpec((ATT_BLOCK, kv_w), lambda b, n: (rb(b, f(n)), col))

    def tab_spec(f):
        return pl.BlockSpec((ATT_BLOCK, hd), lambda b, n: (f(n), 0))

    ident = lambda n: n
    ctx_spec = pl.BlockSpec((1, past, kv_w), lambda b, n: (b, 0, 0))
    return pl.pallas_call(
        functools.partial(_lat_attn_kernel, group=group, hd=hd, n_kv=n_kv),
        grid=(nb, nblk),
        in_specs=[pl.BlockSpec(memory_space=pltpu.SMEM),
                  pl.BlockSpec((ATT_BLOCK, q_w), lambda b, n: (rb(b, n), 0)),
                  kv_spec(kcol, prev), kv_spec(kcol, ident), kv_spec(kcol, nxt),
                  kv_spec(vcol, prev), kv_spec(vcol, ident), kv_spec(vcol, nxt),
                  ctx_spec, ctx_spec,
                  tab_spec(prev), tab_spec(ident), tab_spec(nxt),
                  tab_spec(prev), tab_spec(ident), tab_spec(nxt)],
        out_specs=pl.BlockSpec((ATT_BLOCK, q_w), lambda b, n: (rb(b, n), 0)),
        out_shape=jax.ShapeDtypeStruct((nb * seq, q_w), bf16),
        compiler_params=_params("parallel", "parallel"),
        name="latent_attention",
    )(sink, proj, proj, proj, proj, proj, proj, proj, k_ctx, v_ctx, cos, cos, cos, sin, sin, sin)


RET_HEADS_PER_STEP_CHOICES = (4, 2)


def _ret_kernel(lg_ref, q_ref, k_ref, v_ref, gf_ref, gb_ref, s0f_ref, s0b_ref, o_ref, sf_ref, sb_ref,
                of_scr, ob_scr, *, n_chunks, dk, hps):
    hb = pl.program_id(1)
    c = RET_CHUNK
    ii = lax.broadcasted_iota(jnp.int32, (c, c), 0).astype(f32)
    jj = lax.broadcasted_iota(jnp.int32, (c, c), 1).astype(f32)
    d = ii - jj
    ri = lax.broadcasted_iota(jnp.int32, (c, dk), 0).astype(f32)
    kscale = dk ** -0.5

    consts = []
    for hh in range(hps):
        lgf = lg_ref[0, hb * hps + hh]
        lgb = lg_ref[1, hb * hps + hh]
        fwd = (jnp.where(d >= 0, jnp.exp(jnp.maximum(d, 0.0) * lgf), 0.0),
               jnp.exp((ri + 1.0) * lgf), jnp.exp((c - 1.0 - ri) * lgf), jnp.exp(jnp.full((dk, dk), c * lgf, f32)))
        bwd = (jnp.where(d <= 0, jnp.exp(jnp.maximum(-d, 0.0) * lgb), 0.0),
               jnp.exp((c - ri) * lgb), jnp.exp(ri * lgb), jnp.exp(jnp.full((dk, dk), c * lgb, f32)))
        consts.append((fwd, bwd))

    def body(n, states):
        chains = []
        for hh in range(hps):
            for direction, (chunk_idx, o_scr) in enumerate(((n, of_scr), (n_chunks - 1 - n, ob_scr))):
                decay, qdec, kdec, cd = consts[hh][direction]
                r = pl.ds(pl.multiple_of(chunk_idx * c, c), c)
                cs = slice(hh * dk, (hh + 1) * dk)
                q = q_ref[r, cs]
                k = k_ref[r, cs] * kscale
                vb = v_ref[r, cs].astype(bf16)
                state = states[hh][direction]
                raw = _dot_nt(q.astype(bf16), k.astype(bf16))
                cross = _dot((q * qdec).astype(bf16), state.astype(bf16))
                kv = _dot((k * kdec).T.astype(bf16), vb)
                chains.append((raw, cross, kv, vb, decay, cd, state, o_scr, r, cs))
        scores = [(raw * decay).astype(bf16) for raw, _, _, _, decay, *_ in chains]
        new = []
        for (raw, cross, kv, vb, decay, cd, state, o_scr, r, cs), sc in zip(chains, scores):
            o_scr[r, cs] = _dot(sc, vb) + cross
            new.append(cd * state + kv)
        return tuple((new[2 * hh], new[2 * hh + 1]) for hh in range(hps))

    init = tuple((s0f_ref[0, hh], s0b_ref[0, hh]) for hh in range(hps))
    final = lax.fori_loop(0, n_chunks, body, init)
    for hh in range(hps):
        sf_ref[0, hh] = final[hh][0]
        sb_ref[0, hh] = final[hh][1]

    def hn(o):
        return o * lax.rsqrt(jnp.mean(o * o, axis=-1, keepdims=True) + NORM_EPS)

    def finish(n, carry):
        r = pl.ds(pl.multiple_of(n * c, c), c)
        for hh in range(hps):
            cs = slice(hh * dk, (hh + 1) * dk)
            gf = gf_ref[r, cs]
            gb = gb_ref[r, cs]
            out = hn(of_scr[r, cs]) * (gf * jax.nn.sigmoid(gf)) + hn(ob_scr[r, cs]) * (gb * jax.nn.sigmoid(gb))
            o_ref[r, cs] = out.astype(o_ref.dtype)
        return carry

    lax.fori_loop(0, n_chunks, finish, 0)


def retention(proj, log_gamma, s0f, s0b, nb, seq, col0, n_heads, dk):
    n_chunks = seq // RET_CHUNK
    hps = 1
    for cand in RET_HEADS_PER_STEP_CHOICES:
        if n_heads % cand == 0 and seq * cand * dk * 4 * 12 <= VMEM_LIMIT_BYTES // 2:
            hps = cand
            break
    w = hps * dk
    c0 = col0 // w
    gpb = n_heads // hps

    def col_spec(k):
        return pl.BlockSpec((seq, w), lambda b, h: (b, c0 + k * gpb + h))

    st_spec = pl.BlockSpec((1, hps, dk, dk), lambda b, h: (b, h, 0, 0))
    st_shape = jax.ShapeDtypeStruct((nb, n_heads, dk, dk), f32)
    return pl.pallas_call(
        functools.partial(_ret_kernel, n_chunks=n_chunks, dk=dk, hps=hps),
        grid=(nb, gpb),
        in_specs=[pl.BlockSpec(memory_space=pltpu.SMEM),
                  col_spec(0), col_spec(1), col_spec(2), col_spec(3), col_spec(4),
                  st_spec, st_spec],
        out_specs=[pl.BlockSpec((seq, w), lambda b, h: (b, h)), st_spec, st_spec],
        out_shape=[jax.ShapeDtypeStruct((nb * seq, n_heads * dk), bf16), st_shape, st_shape],
        scratch_shapes=[pltpu.VMEM((seq, w), f32), pltpu.VMEM((seq, w), f32)],
        compiler_params=_params("parallel", "parallel"),
        name="retention",
    )(log_gamma, proj, proj, proj, proj, proj, s0f, s0b)


def _merge_kernel(oa_ref, or_ref, wa_ref, wr_ref, ga_ref, gr_ref, o_ref):
    a = _dot(oa_ref[...], wa_ref[...])
    r = _dot(or_ref[...], wr_ref[...])
    o_ref[...] = (jax.nn.sigmoid(ga_ref[...]) * a + jax.nn.sigmoid(gr_ref[...]) * r).astype(o_ref.dtype)


def merge_branches(o_att, o_ret, w_pa, w_pr, proj, gate_col0, tm, tn):
    ntok, qw = o_att.shape
    rw = o_ret.shape[1]
    d = w_pa.shape[1]
    ga0 = gate_col0 // tn
    gr0 = (gate_col0 + d) // tn
    return pl.pallas_call(
        _merge_kernel,
        grid=(ntok // tm, d // tn),
        in_specs=[pl.BlockSpec((tm, qw), lambda i, j: (i, 0)),
                  pl.BlockSpec((tm, rw), lambda i, j: (i, 0)),
                  pl.BlockSpec((qw, tn), lambda i, j: (0, j)),
                  pl.BlockSpec((rw, tn), lambda i, j: (0, j)),
                  pl.BlockSpec((tm, tn), lambda i, j: (i, ga0 + j)),
                  pl.BlockSpec((tm, tn), lambda i, j: (i, gr0 + j))],
        out_specs=pl.BlockSpec((tm, tn), lambda i, j: (i, j)),
        out_shape=jax.ShapeDtypeStruct((ntok, d), bf16),
        compiler_params=_params("parallel", "arbitrary"),
        name="merge_branches",
    )(o_att, o_ret, w_pa, w_pr, proj, proj)


def _out_proj_kernel(m_ref, w_ref, x_ref, g_ref, o_ref):
    o_ref[...] = x_ref[...] + g_ref[0] * _dot(m_ref[...], w_ref[...])


def out_projection(merged, w_out, x, mod3, row_of_tile, k_gate, tm, tn):
    ntok, d = x.shape
    kd = merged.shape[1]
    return pl.pallas_call(
        _out_proj_kernel,
        grid=(ntok // tm, d // tn),
        in_specs=[pl.BlockSpec((tm, kd), lambda i, j: (i, 0)),
                  pl.BlockSpec((kd, tn), lambda i, j: (0, j)),
                  pl.BlockSpec((tm, tn), lambda i, j: (i, j)),
                  pl.BlockSpec((1, 1, tn), lambda i, j: (row_of_tile(i, tm) * 6 + k_gate, 0, j))],
        out_specs=pl.BlockSpec((tm, tn), lambda i, j: (i, j)),
        out_shape=jax.ShapeDtypeStruct((ntok, d), f32),
        compiler_params=_params("parallel", "arbitrary"),
        name="out_projection",
    )(merged, w_out, x, mod3)


def _extract_round(r, x_ref, rank_ref, vals_ref, pos, sentinel):
    x = x_ref[...]
    m = jnp.max(x, axis=0, keepdims=True)
    idx = jnp.min(jnp.where(x == m, pos, sentinel), axis=0, keepdims=True)
    sel = pos == idx
    rank_ref[...] = jnp.where(sel, lax.convert_element_type(r, f32), rank_ref[...])
    x_ref[...] = jnp.where(sel, -jnp.inf, x)
    vals_ref[pl.ds(r, 1), :] = m


def _candidate_layout(k):
    slabs = []
    for a in range(k // 2):
        n_valid = k // (a + 1)
        slabs.append((a, n_valid, -(-n_valid // SUBLANES) * SUBLANES))
    return slabs


def _peer_route_kernel(q_ref, keys_ref, n1_ref, e1_ref, r2_ref, e2_ref, x1_scr, x2_scr, v1_scr, v2_scr,
                       c_scr, crank_scr, cv_scr, rank2_scr, e2_scr, *, half):
    k = PEER_TOPK
    n_keys, t = x1_scr.shape
    s1 = _dot_nt(keys_ref[0, 0].astype(bf16), q_ref[:, :half].astype(bf16))
    s2 = _dot_nt(keys_ref[0, 1].astype(bf16), q_ref[:, half:].astype(bf16))
    x1_scr[...] = s1
    x2_scr[...] = s2
    e1_ref[0] = jnp.exp(s1 - jnp.max(s1, axis=0, keepdims=True))
    e2_scr[...] = jnp.exp(s2 - jnp.max(s2, axis=0, keepdims=True))
    n1_ref[0] = jnp.full((n_keys, t), float(k), f32)
    rank2_scr[...] = jnp.full((n_keys, t), float(k), f32)
    key_pos = lax.broadcasted_iota(jnp.int32, (n_keys, t), 0).astype(f32)

    def sub_round(r, carry):
        _extract_round(r, x1_scr, n1_ref.at[0], v1_scr, key_pos, float(n_keys))
        _extract_round(r, x2_scr, rank2_scr, v2_scr, key_pos, float(n_keys))
        return carry

    lax.fori_loop(0, k, sub_round, 0)

    v1 = v1_scr[...]
    v2 = v2_scr[...]
    slabs = _candidate_layout(k)
    cands, poss = [], []
    for a, n_valid, n_rows in slabs:
        b = lax.broadcasted_iota(jnp.int32, (n_rows, t), 0)
        cands.append(jnp.where(b < n_valid, v1[a:a + 1, :] + v2[:n_rows, :], -jnp.inf))
        poss.append((a * k + b).astype(f32))
    a_hi = lax.broadcasted_iota(jnp.int32, (k // 2, t), 0) + k // 2
    cands.append(v1[k // 2:, :] + v2[0:1, :])
    poss.append((a_hi * k).astype(f32))
    c_scr[...] = jnp.concatenate(cands, axis=0)
    cand_pos = jnp.concatenate(poss, axis=0)
    crank_scr[...] = jnp.full(c_scr.shape, float(k), f32)

    def cand_round(r, carry):
        _extract_round(r, c_scr, crank_scr, cv_scr, cand_pos, float(k * k))
        return carry

    lax.fori_loop(0, k, cand_round, 0)

    picked = jnp.where(crank_scr[...] < float(k), 1.0, 0.0)
    cnt = []
    row0 = 0
    for a, n_valid, n_rows in slabs:
        cnt.append(jnp.sum(picked[row0:row0 + n_rows, :], axis=0, keepdims=True))
        row0 += n_rows
    for a in range(k // 2, k):
        cnt.append(picked[row0 + a - k // 2:row0 + a - k // 2 + 1, :])
    cvals = cv_scr[...]
    z = jnp.sum(jnp.exp(cvals - cvals[0:1, :]), axis=0, keepdims=True)
    rank1 = n1_ref[0]
    n1 = jnp.zeros_like(rank1)
    for a in range(k):
        n1 = jnp.where(rank1 == float(a), cnt[a], n1)
    n1_ref[0] = n1
    r2_ref[0] = rank2_scr[...].astype(bf16)
    e2_ref[0] = (e2_scr[...] / z).astype(bf16)


def peer_route(qp, sub_keys, t_tile):
    ntok = qp.shape[0]
    heads, _, n_keys, half = sub_keys.shape
    k = PEER_TOPK
    n_cand = sum(s[2] for s in _candidate_layout(k)) + k // 2
    out_spec = pl.BlockSpec((1, n_keys, t_tile), lambda i, h: (h, 0, i))
    out_shape = jax.ShapeDtypeStruct((heads, n_keys, ntok), f32)
    out_shape_b = jax.ShapeDtypeStruct((heads, n_keys, ntok), bf16)
    return pl.pallas_call(
        functools.partial(_peer_route_kernel, half=half),
        grid=(ntok // t_tile, heads),
        in_specs=[pl.BlockSpec((t_tile, 2 * half), lambda i, h: (i, h)),
                  pl.BlockSpec((1, 2, n_keys, half), lambda i, h: (h, 0, 0, 0))],
        out_specs=[out_spec] * 4,
        out_shape=[out_shape, out_shape, out_shape_b, out_shape_b],
        scratch_shapes=[pltpu.VMEM((n_keys, t_tile), f32), pltpu.VMEM((n_keys, t_tile), f32),
                        pltpu.VMEM((k, t_tile), f32), pltpu.VMEM((k, t_tile), f32),
                        pltpu.VMEM((n_cand, t_tile), f32), pltpu.VMEM((n_cand, t_tile), f32),
                        pltpu.VMEM((k, t_tile), f32),
                        pltpu.VMEM((n_keys, t_tile), f32), pltpu.VMEM((n_keys, t_tile), f32)],
        compiler_params=_params("parallel", "parallel"),
        name="peer_route",
    )(qp, sub_keys)


BF16_SUBLANES = 2 * SUBLANES


def _bcast_rows_bf16(row, n_rows):
    tile = jnp.broadcast_to(row, (BF16_SUBLANES, row.shape[1])).astype(bf16)
    return jnp.concatenate([tile] * (n_rows // BF16_SUBLANES), axis=0)


def _peer_dense_kernel(ht_ref, u_ref, v_ref, n1_ref, e1_ref, r2_ref, e2_ref, acc_ref, coef_scr, r2_scr, e2_scr,
                       *, heads, n_keys):
    e = pl.program_id(1)
    et = u_ref.shape[0]
    t = ht_ref.shape[1]
    cpt = et // n_keys

    @pl.when(e == 0)
    def _():
        acc_ref[...] = jnp.zeros_like(acc_ref)
        r2_scr[...] = r2_ref[...]
        e2_scr[...] = e2_ref[...]

    act = _dot(u_ref[...], ht_ref[...])
    for c in range(cpt):
        i = e * cpt + c
        n1_rows = [n1_ref[h, pl.ds(i, 1), :] for h in range(heads)]
        e1_rows = [e1_ref[h, pl.ds(i, 1), :] for h in range(heads)]
        for tk in range(t // LANES):
            cols = slice(tk * LANES, (tk + 1) * LANES)
            gate = None
            for h in range(heads):
                n1 = _bcast_rows_bf16(n1_rows[h][:, cols], n_keys)
                e1 = _bcast_rows_bf16(e1_rows[h][:, cols], n_keys)
                term = jnp.where(r2_scr[h, :, cols] < n1, e2_scr[h, :, cols] * e1, jnp.zeros((), bf16))
                gate = term if gate is None else gate + term
            a = act[c * n_keys:(c + 1) * n_keys, cols].astype(bf16)
            gelu = 0.5 * a * (1.0 + lax.erf(a * (2.0 ** -0.5)))
            coef_scr[c, cols, :] = (gate * gelu).T
    lhs = jnp.concatenate([coef_scr[c] for c in range(cpt)], axis=-1)
    acc_ref[...] += _dot(lhs, v_ref[...])


def peer_dense(h_t, u, v, n1, e1, r2, e2, t_tile, e_tile):
    d, ntok = h_t.shape
    n_exp = u.shape[0]
    heads, n_keys, _ = n1.shape
    once = pl.Buffered(1)
    route_spec = pl.BlockSpec((heads, n_keys, t_tile), lambda t, e: (0, 0, t), pipeline_mode=once)
    return pl.pallas_call(
        functools.partial(_peer_dense_kernel, heads=heads, n_keys=n_keys),
        grid=(ntok // t_tile, n_exp // e_tile),
        in_specs=[pl.BlockSpec((d, t_tile), lambda t, e: (0, t), pipeline_mode=once),
                  pl.BlockSpec((e_tile, d), lambda t, e: (e, 0)),
                  pl.BlockSpec((e_tile, d), lambda t, e: (e, 0)),
                  route_spec, route_spec, route_spec, route_spec],
        out_specs=pl.BlockSpec((t_tile, d), lambda t, e: (t, 0), pipeline_mode=once),
        out_shape=jax.ShapeDtypeStruct((ntok, d), f32),
        scratch_shapes=[pltpu.VMEM((e_tile // n_keys, t_tile, n_keys), bf16),
                        pltpu.VMEM((heads, n_keys, t_tile), bf16),
                        pltpu.VMEM((heads, n_keys, t_tile), bf16)],
        compiler_params=_params("parallel", "arbitrary"),
        name="peer_dense",
    )(h_t, u, v, n1, e1, r2, e2)


def _residual_kernel(x_ref, p_ref, g_ref, o_ref):
    o_ref[...] = x_ref[...] + g_ref[0] * p_ref[...]


def _residual_norm_kernel(x_ref, p_ref, g_ref, w_ref, o_ref):
    x = x_ref[...] + g_ref[0] * p_ref[...]
    y = x * lax.rsqrt(jnp.mean(x * x, axis=-1, keepdims=True) + NORM_EPS)
    o_ref[...] = y * w_ref[...]


def gated_residual(x, p, mod3, row_of_tile, k_gate, tm, norm_w=None):
    ntok, d = x.shape
    row_spec = pl.BlockSpec((tm, d), lambda i: (i, 0))
    in_specs = [row_spec, row_spec, pl.BlockSpec((1, 1, d), lambda i: (row_of_tile(i, tm) * 6 + k_gate, 0, 0))]
    args = [x, p, mod3]
    kern = _residual_kernel
    if norm_w is not None:
        in_specs.append(pl.BlockSpec((1, d), lambda i: (0, 0)))
        args.append(norm_w.reshape(1, d))
        kern = _residual_norm_kernel
    return pl.pallas_call(
        kern,
        grid=(ntok // tm,),
        in_specs=in_specs,
        out_specs=row_spec,
        out_shape=jax.ShapeDtypeStruct((ntok, d), f32),
        compiler_params=_params("parallel"),
        name="gated_residual",
    )(*args)


def _axial_rope_tables(n_tokens, hd):
    rows = n_tokens // GRID_W
    row = jnp.repeat(jnp.arange(rows), GRID_W).astype(f32)
    col = jnp.tile(jnp.arange(GRID_W), rows).astype(f32)
    half = hd // 2
    inv = 1.0 / (ROPE_BASE ** (jnp.arange(0, half, 2, dtype=f32) / half))
    ar = row[:, None] * inv[None, :]
    ac = col[:, None] * inv[None, :]
    ang = jnp.concatenate([ar, ar, ac, ac], axis=-1)
    return jnp.cos(ang), jnp.sin(ang)


def kernel(x_prompt, x_sample, cache_attn_k, cache_attn_v, state_ret_fwd, state_ret_bwd, c, c_ctx, w_ada, b_ada,
           norm_mix, norm_ffn, w_in, attn_sink, ret_decay_logit, w_proj_attn, w_proj_ret, w_out, peer_w_query,
           peer_sub_keys, peer_u, peer_v, norm_final):
    batch, seq, d = x_prompt.shape
    dbatch, dseq, _ = x_sample.shape
    depth = w_in.shape[0]
    assert depth >= 1
    n_heads = attn_sink.shape[1]
    _, _, past, n_kv, hd = cache_attn_k.shape
    ret_heads, ret_dk, ret_dv = state_ret_fwd.shape[2:]
    assert ret_dk == ret_dv
    peer_heads, peer_dkey = peer_w_query.shape[2:]
    q_w = n_heads * hd
    kv_w = n_kv * hd
    r_w = ret_heads * ret_dk
    ret_col0 = q_w + 2 * kv_w
    gate_col0 = ret_col0 + 5 * r_w

    n_prompt = batch * seq
    n_lat = dbatch * dseq

    def cond_row_ctx(i, tm):
        return 0

    def cond_row_lat(i, tm):
        return 1 + i // (dseq // tm)

    n_cond = -(-(dbatch + 1) // SUBLANES) * SUBLANES
    cond = jnp.zeros((n_cond, d), f32).at[0].set(c_ctx).at[1:dbatch + 1].set(c)
    cos, sin = _axial_rope_tables(dseq, hd)
    tm_small = _tile(TM_PROJ, n_prompt, dseq)
    tm_big = _tile(TM_MERGE, n_prompt, dseq)
    tm_res = _tile(TM_RESIDUAL, n_prompt, dseq)
    tn_in = _tile(TN_PROJ, w_in.shape[2])
    tn_gate = _tile(TN_MERGE, d, gate_col0)
    tn_q = _tile(TN_PROJ, peer_heads * peer_dkey)
    e_tile = _tile(E_DENSE, peer_u.shape[1])

    xp = x_prompt.reshape(n_prompt, d)
    xs = x_sample.reshape(n_lat, d)
    new_k, new_v, new_sf, new_sb = [], [], [], []
    for l in range(depth):
        last = l == depth - 1
        mod3 = ada_modulation(cond, w_ada[l], b_ada[l]).reshape(n_cond * 6, 1, d)
        sink = attn_sink[l].astype(f32)
        log_gamma = jax.nn.log_sigmoid(ret_decay_logit[l].astype(f32))
        w_in_b = w_in[l].astype(bf16)
        w_pa_b = w_proj_attn[l].astype(bf16)
        w_pr_b = w_proj_ret[l].astype(bf16)
        w_out_b = w_out[l].astype(bf16)
        wq_b = peer_w_query[l].reshape(d, peer_heads * peer_dkey).astype(bf16)
        u_b = peer_u[l].astype(bf16)
        v_b = peer_v[l].astype(bf16)

        def channel_mix(x, proj, o_att, o_ret, cond_row):
            merged = merge_branches(o_att, o_ret, w_pa_b, w_pr_b, proj, gate_col0, tm_big, tn_gate)
            x = out_projection(merged, w_out_b, x, mod3, cond_row, 2, tm_big, tn_gate)
            qp, h2 = norm_mod_matmul(x, norm_ffn[l], mod3, cond_row, 3, 4, wq_b, tm_small, tn_q, True)
            n1, e1, r2, e2 = peer_route(qp, peer_sub_keys[l], _tile(T_ROUTE, x.shape[0]))
            p = peer_dense(h2.T, u_b, v_b, n1, e1, r2, e2, _tile(T_DENSE, x.shape[0]), e_tile)
            return gated_residual(x, p, mod3, cond_row, 5, tm_res, norm_final if last else None)

        proj = norm_mod_matmul(xp, norm_mix[l], mod3, cond_row_ctx, 0, 1, w_in_b, tm_small, tn_in, False)
        o_att = context_attention(proj, sink, batch, seq, n_heads, n_kv, hd)
        zero_state = jnp.zeros((batch, ret_heads, ret_dk, ret_dv), f32)
        o_ret, s_f, s_b = retention(proj, log_gamma, zero_state, zero_state, batch, seq, ret_col0, ret_heads, ret_dk)
        new_k.append(proj[:, q_w:q_w + kv_w].reshape(batch, seq, n_kv, hd))
        new_v.append(proj[:, q_w + kv_w:q_w + 2 * kv_w].reshape(batch, seq, n_kv, hd))
        new_sf.append(s_f)
        new_sb.append(s_b)
        xp = channel_mix(xp, proj, o_att, o_ret, cond_row_ctx)

        proj = norm_mod_matmul(xs, norm_mix[l], mod3, cond_row_lat, 0, 1, w_in_b, tm_small, tn_in, False)
        o_att = latent_attention(proj, sink, cache_attn_k[:, l].reshape(dbatch, past, kv_w),
                                 cache_attn_v[:, l].reshape(dbatch, past, kv_w), cos, sin,
                                 dbatch, dseq, n_heads, n_kv, hd)
        o_ret, _, _ = retention(proj, log_gamma, state_ret_fwd[:, l], state_ret_bwd[:, l], dbatch, dseq,
                                ret_col0, ret_heads, ret_dk)
        xs = channel_mix(xs, proj, o_att, o_ret, cond_row_lat)

    return (xp.reshape(batch, seq, d), xs.reshape(dbatch, dseq, d), jnp.stack(new_k, axis=1),
            jnp.stack(new_v, axis=1), jnp.stack(new_sf, axis=1), jnp.stack(new_sb, axis=1))
```

```python
import functools

import jax
import jax.numpy as jnp
from jax import lax
from jax.experimental import pallas as pl
from jax.experimental.pallas import tpu as pltpu

GRID_W = 64
WINDOW = 128
ATT_BLOCK = 128
ROPE_BASE = 10000.0
RET_CHUNK = 128
PEER_TOPK = 16
NORM_EPS = 1e-6

LANES = 128
SUBLANES = 8
VMEM_LIMIT_BYTES = 56 * 1024 * 1024
NEG_BIG = -1e30

TM_PROJ, TN_PROJ = 512, 1024
TN_MERGE = 512
TM_RESIDUAL = 256
T_ROUTE = 2048
T_DENSE, E_DENSE = 512, 1024

f32 = jnp.float32
bf16 = jnp.bfloat16


def _params(*sem):
    return pltpu.CompilerParams(dimension_semantics=sem, vmem_limit_bytes=VMEM_LIMIT_BYTES)


def _tile(pref, *dims):
    t = pref
    while any(dim % t for dim in dims):
        t //= 2
    assert t >= LANES
    return t


def _dot(a, b):
    return jnp.dot(a, b, preferred_element_type=f32)


def _dot_nt(a, b):
    return lax.dot_general(a, b, (((1,), (1,)), ((), ())), preferred_element_type=f32)


def _ada_kernel(cond_ref, w_ref, b_ref, o_ref):
    c = cond_ref[...]
    s = (c * jax.nn.sigmoid(c)).astype(bf16)
    o_ref[...] = _dot(s, w_ref[...].astype(bf16)) + b_ref[...]


def ada_modulation(cond, w_ada, b_ada):
    r, d = cond.shape
    n = w_ada.shape[1]
    tn = _tile(TN_MERGE, n)
    return pl.pallas_call(
        _ada_kernel,
        grid=(n // tn,),
        in_specs=[pl.BlockSpec((r, d), lambda j: (0, 0)),
                  pl.BlockSpec((d, tn), lambda j: (0, j)),
                  pl.BlockSpec((1, tn), lambda j: (0, j))],
        out_specs=pl.BlockSpec((r, tn), lambda j: (0, j)),
        out_shape=jax.ShapeDtypeStruct((r, n), f32),
        compiler_params=_params("parallel"),
        name="ada_modulation",
    )(cond, w_ada, b_ada.reshape(1, n))


def _norm_mod(x, g, sh, sc):
    y = x * lax.rsqrt(jnp.mean(x * x, axis=-1, keepdims=True) + NORM_EPS)
    y = y * g
    return y * (1.0 + sc) + sh


def _norm_mod_matmul_kernel(x_ref, g_ref, sh_ref, sc_ref, w_ref, o_ref, h_scr):
    @pl.when(pl.program_id(1) == 0)
    def _():
        h_scr[...] = _norm_mod(x_ref[...], g_ref[...], sh_ref[0], sc_ref[0]).astype(bf16)

    o_ref[...] = _dot(h_scr[...], w_ref[...])


def _norm_mod_matmul_h_kernel(x_ref, g_ref, sh_ref, sc_ref, w_ref, o_ref, h_ref):
    @pl.when(pl.program_id(1) == 0)
    def _():
        h_ref[...] = _norm_mod(x_ref[...], g_ref[...], sh_ref[0], sc_ref[0]).astype(bf16)

    o_ref[...] = _dot(h_ref[...], w_ref[...])


def norm_mod_matmul(x, g, mod3, row_of_tile, k_shift, k_scale, w, tm, tn, emit_h):
    ntok, d = x.shape
    n = w.shape[1]

    def mod_spec(k):
        return pl.BlockSpec((1, 1, d), lambda i, j: (row_of_tile(i, tm) * 6 + k, 0, 0))

    in_specs = [pl.BlockSpec((tm, d), lambda i, j: (i, 0)),
                pl.BlockSpec((1, d), lambda i, j: (0, 0)),
                mod_spec(k_shift), mod_spec(k_scale),
                pl.BlockSpec((d, tn), lambda i, j: (0, j))]
    o_spec = pl.BlockSpec((tm, tn), lambda i, j: (i, j))
    if emit_h:
        return pl.pallas_call(
            _norm_mod_matmul_h_kernel,
            grid=(ntok // tm, n // tn),
            in_specs=in_specs,
            out_specs=[o_spec, pl.BlockSpec((tm, d), lambda i, j: (i, 0))],
            out_shape=[jax.ShapeDtypeStruct((ntok, n), f32), jax.ShapeDtypeStruct((ntok, d), bf16)],
            compiler_params=_params("parallel", "arbitrary"),
            name="norm_mod_matmul_h",
        )(x, g.reshape(1, d), mod3, mod3, w)
    return pl.pallas_call(
        _norm_mod_matmul_kernel,
        grid=(ntok // tm, n // tn),
        in_specs=in_specs,
        out_specs=o_spec,
        out_shape=jax.ShapeDtypeStruct((ntok, n), f32),
        scratch_shapes=[pltpu.VMEM((tm, d), bf16)],
        compiler_params=_params("parallel", "arbitrary"),
        name="norm_mod_matmul",
    )(x, g.reshape(1, d), mod3, mod3, w)


def _softmax_pv(logits, values, sink):
    m = sink
    for lg in logits:
        m = jnp.maximum(m, jnp.max(lg, axis=-1, keepdims=True))
    den = jnp.exp(sink - m)
    acc = None
    for lg, v in zip(logits, values):
        p = jnp.exp(lg - m)
        den = den + jnp.sum(p, axis=-1, keepdims=True)
        pv = _dot(p.astype(bf16), v)
        acc = pv if acc is None else acc + pv
    return acc / den


def _sink_column(sink_ref, kvh, group, rows):
    return jnp.concatenate([jnp.full((rows, 1), sink_ref[kvh * group + g], f32) for g in range(group)], axis=0)


def _ctx_attn_kernel(sink_ref, q_ref, k_ref, v_ref, o_ref, *, group, hd, n_kv):
    scale = hd ** -0.5
    for kvh in range(n_kv):
        k = k_ref[:, kvh * hd:(kvh + 1) * hd].astype(bf16)
        v = v_ref[:, kvh * hd:(kvh + 1) * hd].astype(bf16)
        for g in range(group):
            cols = slice((kvh * group + g) * hd, (kvh * group + g + 1) * hd)
            lg = _dot_nt(q_ref[:, cols].astype(bf16), k) * scale
            o = _softmax_pv([lg], [v], sink_ref[kvh * group + g])
            o_ref[:, cols] = o.astype(o_ref.dtype)


def context_attention(proj, sink, nb, seq, n_heads, n_kv, hd):
    group = n_heads // n_kv
    q_w = n_heads * hd
    kv_w = n_kv * hd
    assert q_w % kv_w == 0
    kcol = q_w // kv_w
    return pl.pallas_call(
        functools.partial(_ctx_attn_kernel, group=group, hd=hd, n_kv=n_kv),
        grid=(nb,),
        in_specs=[pl.BlockSpec(memory_space=pltpu.SMEM),
                  pl.BlockSpec((seq, q_w), lambda b: (b, 0)),
                  pl.BlockSpec((seq, kv_w), lambda b: (b, kcol)),
                  pl.BlockSpec((seq, kv_w), lambda b: (b, kcol + 1))],
        out_specs=pl.BlockSpec((seq, q_w), lambda b: (b, 0)),
        out_shape=jax.ShapeDtypeStruct((nb * seq, q_w), bf16),
        compiler_params=_params("parallel"),
        name="context_attention",
    )(sink, proj, proj, proj)


def _rope(x, cos, sin, first_half):
    rot = jnp.where(first_half, -pltpu.roll(x, 3 * (LANES // 4), 1), pltpu.roll(x, LANES // 4, 1))
    return x * cos + rot * sin


def _lat_attn_kernel(sink_ref, q_ref, kp_ref, kc_ref, kn_ref, vp_ref, vc_ref, vn_ref, kx_ref, vx_ref,
                     cp_ref, cc_ref, cn_ref, sp_ref, sc_ref, sn_ref, o_ref, *, group, hd, n_kv):
    n = pl.program_id(1)
    nblk = pl.num_programs(1)
    scale = hd ** -0.5
    blk = (ATT_BLOCK, hd)
    lane = lax.broadcasted_iota(jnp.int32, blk, 1)
    first_half = (lane % (hd // 2)) < (hd // 4)
    row = lax.broadcasted_iota(jnp.int32, (group * ATT_BLOCK, ATT_BLOCK), 0) % ATT_BLOCK
    col = lax.broadcasted_iota(jnp.int32, (group * ATT_BLOCK, ATT_BLOCK), 1)
    mask_p = jnp.logical_and(col >= row, n > 0)
    mask_n = jnp.logical_and(col <= row, n < nblk - 1)
    for kvh in range(n_kv):
        ks = slice(kvh * hd, (kvh + 1) * hd)
        kp = _rope(kp_ref[:, ks], cp_ref[...], sp_ref[...], first_half).astype(bf16)
        kc = _rope(kc_ref[:, ks], cc_ref[...], sc_ref[...], first_half).astype(bf16)
        kn = _rope(kn_ref[:, ks], cn_ref[...], sn_ref[...], first_half).astype(bf16)
        kx = kx_ref[0, :, ks].astype(bf16)
        values = [vp_ref[:, ks].astype(bf16), vc_ref[:, ks].astype(bf16), vn_ref[:, ks].astype(bf16),
                  vx_ref[0, :, ks].astype(bf16)]
        q0 = kvh * group * hd
        q = jnp.concatenate([_rope(q_ref[:, q0 + g * hd:q0 + (g + 1) * hd], cc_ref[...], sc_ref[...],
                                   first_half).astype(bf16) for g in range(group)], axis=0)
        lp = jnp.where(mask_p, _dot_nt(q, kp) * scale, NEG_BIG)
        lc = _dot_nt(q, kc) * scale
        ln = jnp.where(mask_n, _dot_nt(q, kn) * scale, NEG_BIG)
        lx = _dot_nt(q, kx) * scale
        o = _softmax_pv([lp, lc, ln, lx], values, _sink_column(sink_ref, kvh, group, ATT_BLOCK))
        for g in range(group):
            o_ref[:, q0 + g * hd:q0 + (g + 1) * hd] = o[g * ATT_BLOCK:(g + 1) * ATT_BLOCK].astype(o_ref.dtype)


def latent_attention(proj, sink, k_ctx, v_ctx, cos, sin, nb, seq, n_heads, n_kv, hd):
    assert WINDOW == ATT_BLOCK
    group = n_heads // n_kv
    q_w = n_heads * hd
    kv_w = n_kv * hd
    assert q_w % kv_w == 0
    kcol = q_w // kv_w
    vcol = kcol + 1
    nblk = seq // ATT_BLOCK
    past = k_ctx.shape[1]

    def rb(b, n):
        return b * nblk + n

    def prev(n):
        return jnp.maximum(n - 1, 0)

    def nxt(n):
        return jnp.minimum(n + 1, nblk - 1)

    def kv_spec(col, f):
        return pl.BlockSpec((ATT_BLOCK, kv_w), lambda b, n: (rb(b, f(n)), col))

    def tab_spec(f):
        return pl.BlockSpec((ATT_BLOCK, hd), lambda b, n: (f(n), 0))

    ident = lambda n: n
    ctx_spec = pl.BlockSpec((1, past, kv_w), lambda b, n: (b, 0, 0))
    return pl.pallas_call(
        functools.partial(_lat_attn_kernel, group=group, hd=hd, n_kv=n_kv),
        grid=(nb, nblk),
        in_specs=[pl.BlockSpec(memory_space=pltpu.SMEM),
                  pl.BlockSpec((ATT_BLOCK, q_w), lambda b, n: (rb(b, n), 0)),
                  kv_spec(kcol, prev), kv_spec(kcol, ident), kv_spec(kcol, nxt),
                  kv_spec(vcol, prev), kv_spec(vcol, ident), kv_spec(vcol, nxt),
                  ctx_spec, ctx_spec,
                  tab_spec(prev), tab_spec(ident), tab_spec(nxt),
                  tab_spec(prev), tab_spec(ident), tab_spec(nxt)],
        out_specs=pl.BlockSpec((ATT_BLOCK, q_w), lambda b, n: (rb(b, n), 0)),
        out_shape=jax.ShapeDtypeStruct((nb * seq, q_w), bf16),
        compiler_params=_params("parallel", "parallel"),
        name="latent_attention",
    )(sink, proj, proj, proj, proj, proj, proj, proj, k_ctx, v_ctx, cos, cos, cos, sin, sin, sin)


RET_HEADS_PER_STEP_CHOICES = (4, 2)


def _ret_kernel(lg_ref, q_ref, k_ref, v_ref, gf_ref, gb_ref, s0f_ref, s0b_ref, o_ref, sf_ref, sb_ref,
                of_scr, ob_scr, *, n_chunks, dk, hps):
    hb = pl.program_id(1)
    c = RET_CHUNK
    ii = lax.broadcasted_iota(jnp.int32, (c, c), 0).astype(f32)
    jj = lax.broadcasted_iota(jnp.int32, (c, c), 1).astype(f32)
    d = ii - jj
    ri = lax.broadcasted_iota(jnp.int32, (c, dk), 0).astype(f32)
    kscale = dk ** -0.5

    consts = []
    for hh in range(hps):
        lgf = lg_ref[0, hb * hps + hh]
        lgb = lg_ref[1, hb * hps + hh]
        fwd = (jnp.where(d >= 0, jnp.exp(jnp.maximum(d, 0.0) * lgf), 0.0),
               jnp.exp((ri + 1.0) * lgf), jnp.exp((c - 1.0 - ri) * lgf), jnp.exp(jnp.full((dk, dk), c * lgf, f32)))
        bwd = (jnp.where(d <= 0, jnp.exp(jnp.maximum(-d, 0.0) * lgb), 0.0),
               jnp.exp((c - ri) * lgb), jnp.exp(ri * lgb), jnp.exp(jnp.full((dk, dk), c * lgb, f32)))
        consts.append((fwd, bwd))

    def body(n, states):
        chains = []
        for hh in range(hps):
            for direction, (chunk_idx, o_scr) in enumerate(((n, of_scr), (n_chunks - 1 - n, ob_scr))):
                decay, qdec, kdec, cd = consts[hh][direction]
                r = pl.ds(pl.multiple_of(chunk_idx * c, c), c)
                cs = slice(hh * dk, (hh + 1) * dk)
                q = q_ref[r, cs]
                k = k_ref[r, cs] * kscale
                vb = v_ref[r, cs].astype(bf16)
                state = states[hh][direction]
                raw = _dot_nt(q.astype(bf16), k.astype(bf16))
                cross = _dot((q * qdec).astype(bf16), state.astype(bf16))
                kv = _dot((k * kdec).T.astype(bf16), vb)
                chains.append((raw, cross, kv, vb, decay, cd, state, o_scr, r, cs))
        scores = [(raw * decay).astype(bf16) for raw, _, _, _, decay, *_ in chains]
        new = []
        for (raw, cross, kv, vb, decay, cd, state, o_scr, r, cs), sc in zip(chains, scores):
            o_scr[r, cs] = _dot(sc, vb) + cross
            new.append(cd * state + kv)
        return tuple((new[2 * hh], new[2 * hh + 1]) for hh in range(hps))

    init = tuple((s0f_ref[0, hh], s0b_ref[0, hh]) for hh in range(hps))
    final = lax.fori_loop(0, n_chunks, body, init)
    for hh in range(hps):
        sf_ref[0, hh] = final[hh][0]
        sb_ref[0, hh] = final[hh][1]

    def hn(o):
        return o * lax.rsqrt(jnp.mean(o * o, axis=-1, keepdims=True) + NORM_EPS)

    def finish(n, carry):
        r = pl.ds(pl.multiple_of(n * c, c), c)
        for hh in range(hps):
            cs = slice(hh * dk, (hh + 1) * dk)
            gf = gf_ref[r, cs]
            gb = gb_ref[r, cs]
            out = hn(of_scr[r, cs]) * (gf * jax.nn.sigmoid(gf)) + hn(ob_scr[r, cs]) * (gb * jax.nn.sigmoid(gb))
            o_ref[r, cs] = out.astype(o_ref.dtype)
        return carry

    lax.fori_loop(0, n_chunks, finish, 0)


def retention(proj, log_gamma, s0f, s0b, nb, seq, col0, n_heads, dk):
    n_chunks = seq // RET_CHUNK
    hps = 1
    for cand in RET_HEADS_PER_STEP_CHOICES:
        if n_heads % cand == 0 and seq * cand * dk * 4 * 12 <= VMEM_LIMIT_BYTES // 2:
            hps = cand
            break
    w = hps * dk
    c0 = col0 // w
    gpb = n_heads // hps

    def col_spec(k):
        return pl.BlockSpec((seq, w), lambda b, h: (b, c0 + k * gpb + h))

    st_spec = pl.BlockSpec((1, hps, dk, dk), lambda b, h: (b, h, 0, 0))
    st_shape = jax.ShapeDtypeStruct((nb, n_heads, dk, dk), f32)
    return pl.pallas_call(
        functools.partial(_ret_kernel, n_chunks=n_chunks, dk=dk, hps=hps),
        grid=(nb, gpb),
        in_specs=[pl.BlockSpec(memory_space=pltpu.SMEM),
                  col_spec(0), col_spec(1), col_spec(2), col_spec(3), col_spec(4),
                  st_spec, st_spec],
        out_specs=[pl.BlockSpec((seq, w), lambda b, h: (b, h)), st_spec, st_spec],
        out_shape=[jax.ShapeDtypeStruct((nb * seq, n_heads * dk), bf16), st_shape, st_shape],
        scratch_shapes=[pltpu.VMEM((seq, w), f32), pltpu.VMEM((seq, w), f32)],
        compiler_params=_params("parallel", "parallel"),
        name="retention",
    )(log_gamma, proj, proj, proj, proj, proj, s0f, s0b)


def _merge_out_kernel(oa_ref, or_ref, wa_ref, wr_ref, ga_ref, gr_ref, wo_ref, x_ref, g_ref, o_ref, m_scr):
    j = pl.program_id(1)
    n = m_scr.shape[0]

    @pl.when(j < n)
    def _():
        a = _dot(oa_ref[...], wa_ref[...])
        r = _dot(or_ref[...], wr_ref[...])
        m_scr[j] = (jax.nn.sigmoid(ga_ref[...]) * a + jax.nn.sigmoid(gr_ref[...]) * r).astype(bf16)

    @pl.when(j >= n)
    def _():
        merged = jnp.concatenate([m_scr[k] for k in range(n)], axis=-1)
        o_ref[...] = x_ref[...] + g_ref[0] * _dot(merged, wo_ref[...])


def merge_out_projection(o_att, o_ret, w_pa, w_pr, proj, gate_col0, w_out, x, mod3, row_of_tile, k_gate, tm, tn):
    ntok, qw = o_att.shape
    rw = o_ret.shape[1]
    d = w_pa.shape[1]
    n = d // tn
    ga0 = gate_col0 // tn
    gr0 = (gate_col0 + d) // tn

    def jm(j):
        return jnp.minimum(j, n - 1)

    def jo(j):
        return jnp.maximum(j - n, 0)

    return pl.pallas_call(
        _merge_out_kernel,
        grid=(ntok // tm, 2 * n),
        in_specs=[pl.BlockSpec((tm, qw), lambda i, j: (i, 0)),
                  pl.BlockSpec((tm, rw), lambda i, j: (i, 0)),
                  pl.BlockSpec((qw, tn), lambda i, j: (0, jm(j))),
                  pl.BlockSpec((rw, tn), lambda i, j: (0, jm(j))),
                  pl.BlockSpec((tm, tn), lambda i, j: (i, ga0 + jm(j))),
                  pl.BlockSpec((tm, tn), lambda i, j: (i, gr0 + jm(j))),
                  pl.BlockSpec((d, tn), lambda i, j: (0, jo(j))),
                  pl.BlockSpec((tm, tn), lambda i, j: (i, jo(j))),
                  pl.BlockSpec((1, 1, tn), lambda i, j: (row_of_tile(i, tm) * 6 + k_gate, 0, jo(j)))],
        out_specs=pl.BlockSpec((tm, tn), lambda i, j: (i, jo(j))),
        out_shape=jax.ShapeDtypeStruct((ntok, d), f32),
        scratch_shapes=[pltpu.VMEM((n, tm, tn), bf16)],
        compiler_params=_params("parallel", "arbitrary"),
        name="merge_out_projection",
    )(o_att, o_ret, w_pa, w_pr, proj, proj, w_out, x, mod3)


def _extract_round(r, x_ref, rank_ref, vals_ref, pos, sentinel):
    x = x_ref[...]
    m = jnp.max(x, axis=0, keepdims=True)
    idx = jnp.min(jnp.where(x == m, pos, sentinel), axis=0, keepdims=True)
    sel = pos == idx
    rank_ref[...] = jnp.where(sel, lax.convert_element_type(r, f32), rank_ref[...])
    x_ref[...] = jnp.where(sel, -jnp.inf, x)
    vals_ref[pl.ds(r, 1), :] = m


def _candidate_layout(k):
    slabs = []
    for a in range(k // 2):
        n_valid = k // (a + 1)
        slabs.append((a, n_valid, -(-n_valid // SUBLANES) * SUBLANES))
    return slabs


def _peer_route_kernel(q_ref, keys_ref, n1_ref, e1_ref, r2_ref, e2_ref, x1_scr, x2_scr, v1_scr, v2_scr,
                       c_scr, crank_scr, cv_scr, rank2_scr, e2_scr, *, half):
    k = PEER_TOPK
    n_keys, t = x1_scr.shape
    s1 = _dot_nt(keys_ref[0, 0].astype(bf16), q_ref[:, :half].astype(bf16))
    s2 = _dot_nt(keys_ref[0, 1].astype(bf16), q_ref[:, half:].astype(bf16))
    x1_scr[...] = s1
    x2_scr[...] = s2
    e1_ref[0] = jnp.exp(s1 - jnp.max(s1, axis=0, keepdims=True))
    e2_scr[...] = jnp.exp(s2 - jnp.max(s2, axis=0, keepdims=True))
    n1_ref[0] = jnp.full((n_keys, t), float(k), f32)
    rank2_scr[...] = jnp.full((n_keys, t), float(k), f32)
    key_pos = lax.broadcasted_iota(jnp.int32, (n_keys, t), 0).astype(f32)

    def sub_round(r, carry):
        _extract_round(r, x1_scr, n1_ref.at[0], v1_scr, key_pos, float(n_keys))
        _extract_round(r, x2_scr, rank2_scr, v2_scr, key_pos, float(n_keys))
        return carry

    lax.fori_loop(0, k, sub_round, 0)

    v1 = v1_scr[...]
    v2 = v2_scr[...]
    slabs = _candidate_layout(k)
    cands, poss = [], []
    for a, n_valid, n_rows in slabs:
        b = lax.broadcasted_iota(jnp.int32, (n_rows, t), 0)
        cands.append(jnp.where(b < n_valid, v1[a:a + 1, :] + v2[:n_rows, :], -jnp.inf))
        poss.append((a * k + b).astype(f32))
    a_hi = lax.broadcasted_iota(jnp.int32, (k // 2, t), 0) + k // 2
    cands.append(v1[k // 2:, :] + v2[0:1, :])
    poss.append((a_hi * k).astype(f32))
    c_scr[...] = jnp.concatenate(cands, axis=0)
    cand_pos = jnp.concatenate(poss, axis=0)
    crank_scr[...] = jnp.full(c_scr.shape, float(k), f32)

    def cand_round(r, carry):
        _extract_round(r, c_scr, crank_scr, cv_scr, cand_pos, float(k * k))
        return carry

    lax.fori_loop(0, k, cand_round, 0)

    picked = jnp.where(crank_scr[...] < float(k), 1.0, 0.0)
    cnt = []
    row0 = 0
    for a, n_valid, n_rows in slabs:
        cnt.append(jnp.sum(picked[row0:row0 + n_rows, :], axis=0, keepdims=True))
        row0 += n_rows
    for a in range(k // 2, k):
        cnt.append(picked[row0 + a - k // 2:row0 + a - k // 2 + 1, :])
    cvals = cv_scr[...]
    z = jnp.sum(jnp.exp(cvals - cvals[0:1, :]), axis=0, keepdims=True)
    rank1 = n1_ref[0]
    n1 = jnp.zeros_like(rank1)
    for a in range(k):
        n1 = jnp.where(rank1 == float(a), cnt[a], n1)
    n1_ref[0] = n1
    r2_ref[0] = rank2_scr[...].astype(bf16)
    e2_ref[0] = (e2_scr[...] / z).astype(bf16)


def peer_route(qp, sub_keys, t_tile):
    ntok = qp.shape[0]
    heads, _, n_keys, half = sub_keys.shape
    k = PEER_TOPK
    n_cand = sum(s[2] for s in _candidate_layout(k)) + k // 2
    out_spec = pl.BlockSpec((1, n_keys, t_tile), lambda i, h: (h, 0, i))
    out_shape = jax.ShapeDtypeStruct((heads, n_keys, ntok), f32)
    out_shape_b = jax.ShapeDtypeStruct((heads, n_keys, ntok), bf16)
    return pl.pallas_call(
        functools.partial(_peer_route_kernel, half=half),
        grid=(ntok // t_tile, heads),
        in_specs=[pl.BlockSpec((t_tile, 2 * half), lambda i, h: (i, h)),
                  pl.BlockSpec((1, 2, n_keys, half), lambda i, h: (h, 0, 0, 0))],
        out_specs=[out_spec] * 4,
        out_shape=[out_shape, out_shape, out_shape_b, out_shape_b],
        scratch_shapes=[pltpu.VMEM((n_keys, t_tile), f32), pltpu.VMEM((n_keys, t_tile), f32),
                        pltpu.VMEM((k, t_tile), f32), pltpu.VMEM((k, t_tile), f32),
                        pltpu.VMEM((n_cand, t_tile), f32), pltpu.VMEM((n_cand, t_tile), f32),
                        pltpu.VMEM((k, t_tile), f32),
                        pltpu.VMEM((n_keys, t_tile), f32), pltpu.VMEM((n_keys, t_tile), f32)],
        compiler_params=_params("parallel", "parallel"),
        name="peer_route",
    )(qp, sub_keys)


BF16_SUBLANES = 2 * SUBLANES


def _bcast_rows_bf16(row, n_rows):
    tile = jnp.broadcast_to(row, (BF16_SUBLANES, row.shape[1])).astype(bf16)
    return jnp.concatenate([tile] * (n_rows // BF16_SUBLANES), axis=0)


def _peer_dense_kernel(ht_ref, u_ref, v_ref, n1_ref, e1_ref, r2_ref, e2_ref, acc_ref, coef_scr, r2_scr, e2_scr,
                       *, heads, n_keys):
    e = pl.program_id(1)
    et = u_ref.shape[0]
    t = ht_ref.shape[1]
    cpt = et // n_keys

    @pl.when(e == 0)
    def _():
        acc_ref[...] = jnp.zeros_like(acc_ref)
        r2_scr[...] = r2_ref[...]
        e2_scr[...] = e2_ref[...]

    act = _dot(u_ref[...], ht_ref[...])
    for c in range(cpt):
        i = e * cpt + c
        n1_rows = [n1_ref[h, pl.ds(i, 1), :] for h in range(heads)]
        e1_rows = [e1_ref[h, pl.ds(i, 1), :] for h in range(heads)]
        for tk in range(t // LANES):
            cols = slice(tk * LANES, (tk + 1) * LANES)
            gate = None
            for h in range(heads):
                n1 = _bcast_rows_bf16(n1_rows[h][:, cols], n_keys)
                e1 = _bcast_rows_bf16(e1_rows[h][:, cols], n_keys)
                term = jnp.where(r2_scr[h, :, cols] < n1, e2_scr[h, :, cols] * e1, jnp.zeros((), bf16))
                gate = term if gate is None else gate + term
            a = act[c * n_keys:(c + 1) * n_keys, cols].astype(bf16)
            gelu = 0.5 * a * (1.0 + lax.erf(a * (2.0 ** -0.5)))
            coef_scr[c, cols, :] = (gate * gelu).T
    lhs = jnp.concatenate([coef_scr[c] for c in range(cpt)], axis=-1)
    acc_ref[...] += _dot(lhs, v_ref[...])


def peer_dense(h_t, u, v, n1, e1, r2, e2, t_tile, e_tile):
    d, ntok = h_t.shape
    n_exp = u.shape[0]
    heads, n_keys, _ = n1.shape
    once = pl.Buffered(1)
    route_spec = pl.BlockSpec((heads, n_keys, t_tile), lambda t, e: (0, 0, t), pipeline_mode=once)
    return pl.pallas_call(
        functools.partial(_peer_dense_kernel, heads=heads, n_keys=n_keys),
        grid=(ntok // t_tile, n_exp // e_tile),
        in_specs=[pl.BlockSpec((d, t_tile), lambda t, e: (0, t), pipeline_mode=once),
                  pl.BlockSpec((e_tile, d), lambda t, e: (e, 0)),
                  pl.BlockSpec((e_tile, d), lambda t, e: (e, 0)),
                  route_spec, route_spec, route_spec, route_spec],
        out_specs=pl.BlockSpec((t_tile, d), lambda t, e: (t, 0), pipeline_mode=once),
        out_shape=jax.ShapeDtypeStruct((ntok, d), f32),
        scratch_shapes=[pltpu.VMEM((e_tile // n_keys, t_tile, n_keys), bf16),
                        pltpu.VMEM((heads, n_keys, t_tile), bf16),
                        pltpu.VMEM((heads, n_keys, t_tile), bf16)],
        compiler_params=_params("parallel", "arbitrary"),
        name="peer_dense",
    )(h_t, u, v, n1, e1, r2, e2)


def _residual_kernel(x_ref, p_ref, g_ref, o_ref):
    o_ref[...] = x_ref[...] + g_ref[0] * p_ref[...]


def _residual_norm_kernel(x_ref, p_ref, g_ref, w_ref, o_ref):
    x = x_ref[...] + g_ref[0] * p_ref[...]
    y = x * lax.rsqrt(jnp.mean(x * x, axis=-1, keepdims=True) + NORM_EPS)
    o_ref[...] = y * w_ref[...]


def gated_residual(x, p, mod3, row_of_tile, k_gate, tm, norm_w=None):
    ntok, d = x.shape
    row_spec = pl.BlockSpec((tm, d), lambda i: (i, 0))
    in_specs = [row_spec, row_spec, pl.BlockSpec((1, 1, d), lambda i: (row_of_tile(i, tm) * 6 + k_gate, 0, 0))]
    args = [x, p, mod3]
    kern = _residual_kernel
    if norm_w is not None:
        in_specs.append(pl.BlockSpec((1, d), lambda i: (0, 0)))
        args.append(norm_w.reshape(1, d))
        kern = _residual_norm_kernel
    return pl.pallas_call(
        kern,
        grid=(ntok // tm,),
        in_specs=in_specs,
        out_specs=row_spec,
        out_shape=jax.ShapeDtypeStruct((ntok, d), f32),
        compiler_params=_params("parallel"),
        name="gated_residual",
    )(*args)


def _axial_rope_tables(n_tokens, hd):
    rows = n_tokens // GRID_W
    row = jnp.repeat(jnp.arange(rows), GRID_W).astype(f32)
    col = jnp.tile(jnp.arange(GRID_W), rows).astype(f32)
    half = hd // 2
    inv = 1.0 / (ROPE_BASE ** (jnp.arange(0, half, 2, dtype=f32) / half))
    ar = row[:, None] * inv[None, :]
    ac = col[:, None] * inv[None, :]
    ang = jnp.concatenate([ar, ar, ac, ac], axis=-1)
    return jnp.cos(ang), jnp.sin(ang)


def kernel(x_prompt, x_sample, cache_attn_k, cache_attn_v, state_ret_fwd, state_ret_bwd, c, c_ctx, w_ada, b_ada,
           norm_mix, norm_ffn, w_in, attn_sink, ret_decay_logit, w_proj_attn, w_proj_ret, w_out, peer_w_query,
           peer_sub_keys, peer_u, peer_v, norm_final):
    batch, seq, d = x_prompt.shape
    dbatch, dseq, _ = x_sample.shape
    depth = w_in.shape[0]
    assert depth >= 1
    n_heads = attn_sink.shape[1]
    _, _, past, n_kv, hd = cache_attn_k.shape
    ret_heads, ret_dk, ret_dv = state_ret_fwd.shape[2:]
    assert ret_dk == ret_dv
    peer_heads, peer_dkey = peer_w_query.shape[2:]
    q_w = n_heads * hd
    kv_w = n_kv * hd
    r_w = ret_heads * ret_dk
    ret_col0 = q_w + 2 * kv_w
    gate_col0 = ret_col0 + 5 * r_w

    n_prompt = batch * seq
    n_lat = dbatch * dseq

    def cond_row_ctx(i, tm):
        return 0

    def cond_row_lat(i, tm):
        return 1 + i // (dseq // tm)

    n_cond = -(-(dbatch + 1) // SUBLANES) * SUBLANES
    cond = jnp.zeros((n_cond, d), f32).at[0].set(c_ctx).at[1:dbatch + 1].set(c)
    cos, sin = _axial_rope_tables(dseq, hd)
    tm_small = _tile(TM_PROJ, n_prompt, dseq)
    tm_res = _tile(TM_RESIDUAL, n_prompt, dseq)
    tn_in = _tile(TN_PROJ, w_in.shape[2])
    tn_gate = _tile(TN_MERGE, d, gate_col0)
    tn_q = _tile(TN_PROJ, peer_heads * peer_dkey)
    e_tile = _tile(E_DENSE, peer_u.shape[1])

    xp = x_prompt.reshape(n_prompt, d)
    xs = x_sample.reshape(n_lat, d)
    new_k, new_v, new_sf, new_sb = [], [], [], []
    for l in range(depth):
        last = l == depth - 1
        mod3 = ada_modulation(cond, w_ada[l], b_ada[l]).reshape(n_cond * 6, 1, d)
        sink = attn_sink[l].astype(f32)
        log_gamma = jax.nn.log_sigmoid(ret_decay_logit[l].astype(f32))
        w_in_b = w_in[l].astype(bf16)
        w_pa_b = w_proj_attn[l].astype(bf16)
        w_pr_b = w_proj_ret[l].astype(bf16)
        w_out_b = w_out[l].astype(bf16)
        wq_b = peer_w_query[l].reshape(d, peer_heads * peer_dkey).astype(bf16)
        u_b = peer_u[l].astype(bf16)
        v_b = peer_v[l].astype(bf16)

        def channel_mix(x, proj, o_att, o_ret, cond_row):
            x = merge_out_projection(o_att, o_ret, w_pa_b, w_pr_b, proj, gate_col0, w_out_b, x, mod3, cond_row, 2,
                                     tm_small, tn_gate)
            qp, h2 = norm_mod_matmul(x, norm_ffn[l], mod3, cond_row, 3, 4, wq_b, tm_small, tn_q, True)
            n1, e1, r2, e2 = peer_route(qp, peer_sub_keys[l], _tile(T_ROUTE, x.shape[0]))
            p = peer_dense(h2.T, u_b, v_b, n1, e1, r2, e2, _tile(T_DENSE, x.shape[0]), e_tile)
            return gated_residual(x, p, mod3, cond_row, 5, tm_res, norm_final if last else None)

        proj = norm_mod_matmul(xp, norm_mix[l], mod3, cond_row_ctx, 0, 1, w_in_b, tm_small, tn_in, False)
        o_att = context_attention(proj, sink, batch, seq, n_heads, n_kv, hd)
        zero_state = jnp.zeros((batch, ret_heads, ret_dk, ret_dv), f32)
        o_ret, s_f, s_b = retention(proj, log_gamma, zero_state, zero_state, batch, seq, ret_col0, ret_heads, ret_dk)
        new_k.append(proj[:, q_w:q_w + kv_w].reshape(batch, seq, n_kv, hd))
        new_v.append(proj[:, q_w + kv_w:q_w + 2 * kv_w].reshape(batch, seq, n_kv, hd))
        new_sf.append(s_f)
        new_sb.append(s_b)
        xp = channel_mix(xp, proj, o_att, o_ret, cond_row_ctx)

        proj = norm_mod_matmul(xs, norm_mix[l], mod3, cond_row_lat, 0, 1, w_in_b, tm_small, tn_in, False)
        o_att = latent_attention(proj, sink, cache_attn_k[:, l].reshape(dbatch, past, kv_w),
                                 cache_attn_v[:, l].reshape(dbatch, past, kv_w), cos, sin,
                                 dbatch, dseq, n_heads, n_kv, hd)
        o_ret, _, _ = retention(proj, log_gamma, state_ret_fwd[:, l], state_ret_bwd[:, l], dbatch, dseq,
                                ret_col0, ret_heads, ret_dk)
        xs = channel_mix(xs, proj, o_att, o_ret, cond_row_lat)

    return (xp.reshape(batch, seq, d), xs.reshape(dbatch, dseq, d), jnp.stack(new_k, axis=1),
            jnp.stack(new_v, axis=1), jnp.stack(new_sf, axis=1), jnp.stack(new_sb, axis=1))
```
